```python
import jax, jax.numpy as jnp
from jax import lax
import numpy as np

D_MODEL = 2048
BATCH = 2
SEQ = 8192
DEPTH = 4
DEC_BATCH = 16
DEC_SEQ = 16
PAST_LEN = 1024

CHUNK = 64
N_LEFT_CHUNKS = 8
WINDOW = N_LEFT_CHUNKS * CHUNK
BAND = (N_LEFT_CHUNKS + 1) * CHUNK
N_HEADS = 16
HEAD_DIM = D_MODEL // N_HEADS
MAX_REL = 128
N_REL = 2 * MAX_REL + 1
D_FF = -(-8 * D_MODEL // (3 * 256)) * 256
Q_BLOCK = 128
N_A_LAYERS = (DEPTH + 1) // 2
N_B_LAYERS = DEPTH // 2
FORGET_BIAS_INIT = 3.0
EPS = 1e-6
SCALE = HEAD_DIM ** -0.5
NEG = -1e30

kernel_name = "hybrid_chunkband_fox_stream_step"


def rms_norm(x, g):
    xf = x.astype(jnp.float32)
    y = xf * lax.rsqrt(jnp.mean(xf * xf, axis=-1, keepdims=True) + EPS)
    return (y * g.astype(jnp.float32)).astype(x.dtype)


def qkv_heads(h, w_qkv, g_q, g_k):
    b, s, _ = h.shape
    qkv = (h @ w_qkv).reshape(b, s, 3, N_HEADS, HEAD_DIM)
    return rms_norm(qkv[:, :, 0], g_q), rms_norm(qkv[:, :, 1], g_k), qkv[:, :, 2]


def masked_softmax(logits, mask):
    return jax.nn.softmax(jnp.where(mask, logits, NEG), axis=-1)


def rel_bias(table, q_pos, k_pos):
    d = jnp.clip(q_pos[:, None] - k_pos[None, :], -MAX_REL, MAX_REL) + MAX_REL
    return table[:, d].astype(jnp.float32)


def band_attention_prompt(q, k, v, table):
    b, s, H, hd = q.shape
    nc = s // CHUNK
    qc = q.reshape(b, nc, CHUNK, H, hd)
    pad = jnp.zeros((b, WINDOW, H, hd), k.dtype)
    kp = jnp.concatenate([pad, k], axis=1).reshape(b, nc + N_LEFT_CHUNKS, CHUNK, H, hd)
    vp = jnp.concatenate([pad, v], axis=1).reshape(b, nc + N_LEFT_CHUNKS, CHUNK, H, hd)
    idx = jnp.arange(nc)[:, None] + jnp.arange(N_LEFT_CHUNKS + 1)[None, :]
    kb = kp[:, idx].reshape(b, nc, BAND, H, hd)
    vb = vp[:, idx].reshape(b, nc, BAND, H, hd)
    bias = rel_bias(table, WINDOW + jnp.arange(CHUNK), jnp.arange(BAND))
    logits = jnp.einsum('bcqhd,bckhd->bchqk', qc, kb,
                        preferred_element_type=jnp.float32) * SCALE + bias[None, None]
    key_valid = jnp.repeat(idx >= N_LEFT_CHUNKS, CHUNK, axis=1)
    p = masked_softmax(logits, key_valid[None, :, None, None, :])
    out = jnp.einsum('bchqk,bckhd->bcqhd', p.astype(v.dtype), vb)
    return out.reshape(b, s, H * hd)


def band_attention_sample(q, k_new, v_new, k_cache, v_cache, table):
    L = k_cache.shape[1]
    b, n, H, hd = q.shape
    k = jnp.concatenate([k_cache.astype(k_new.dtype), k_new], axis=1)
    v = jnp.concatenate([v_cache.astype(v_new.dtype), v_new], axis=1)
    bias = rel_bias(table, L + jnp.arange(n), jnp.arange(L + n))
    logits = jnp.einsum('bqhd,bkhd->bhqk', q, k,
                        preferred_element_type=jnp.float32) * SCALE + bias[None]
    p = jax.nn.softmax(logits, axis=-1)
    out = jnp.einsum('bhqk,bkhd->bqhd', p.astype(v.dtype), v)
    return out.reshape(b, n, H * hd)


def log_forget(h, w_f, b_f):
    return jax.nn.log_sigmoid((h @ w_f).astype(jnp.float32) + b_f.astype(jnp.float32))


def forgetting_attention_prompt(q, k, v, logf):
    b, s, H, hd = q.shape
    nb = s // Q_BLOCK
    c = jnp.cumsum(logf, axis=1).transpose(0, 2, 1)
    qb = q.reshape(b, nb, Q_BLOCK, H, hd).transpose(1, 0, 2, 3, 4)
    cb = c.reshape(b, H, nb, Q_BLOCK).transpose(2, 0, 1, 3)
    starts = jnp.arange(nb) * Q_BLOCK
    k_pos = jnp.arange(s)

    def block(args):
        qi, ci, start = args
        logits = jnp.einsum('bqhd,bkhd->bhqk', qi, k, preferred_element_type=jnp.float32) * SCALE
        logits = logits + ci[..., None] - c[:, :, None, :]
        mask = k_pos[None, :] <= (start + jnp.arange(Q_BLOCK))[:, None]
        p = masked_softmax(logits, mask)
        return jnp.einsum('bhqk,bkhd->bqhd', p.astype(v.dtype), v)

    out = lax.map(block, (qb, cb, starts))
    return out.transpose(1, 0, 2, 3, 4).reshape(b, s, H * hd)


def forgetting_attention_sample(q, k_new, v_new, logf_new, k_cache, v_cache, logf_cache):
    L = k_cache.shape[1]
    b, n, H, hd = q.shape
    k = jnp.concatenate([k_cache.astype(k_new.dtype), k_new], axis=1)
    v = jnp.concatenate([v_cache.astype(v_new.dtype), v_new], axis=1)
    c = jnp.cumsum(jnp.concatenate([logf_cache.astype(jnp.float32), logf_new], axis=1),
                   axis=1).transpose(0, 2, 1)
    logits = jnp.einsum('bqhd,bkhd->bhqk', q, k, preferred_element_type=jnp.float32) * SCALE
    logits = logits + c[:, :, L:, None] - c[:, :, None, :]
    mask = jnp.arange(L + n)[None, :] <= (L + jnp.arange(n))[:, None]
    p = masked_softmax(logits, mask)
    out = jnp.einsum('bhqk,bkhd->bqhd', p.astype(v.dtype), v)
    return out.reshape(b, n, H * hd)


def swiglu(h, w_gate, w_up, w_down):
    return (jax.nn.silu(h @ w_gate) * (h @ w_up)) @ w_down


def setup_inputs(seed: int = 0) -> dict:
    key = jax.random.key(seed)
    ks = jax.random.split(key, 20)
    D, H, hd, F = D_MODEL, N_HEADS, HEAD_DIM, D_FF
    la = min(WINDOW, PAST_LEN)
    f32 = jnp.float32
    nrm = jax.random.normal
    return {
        'x_prompt': nrm(ks[0], (BATCH, SEQ, D), f32),
        'x_sample': nrm(ks[1], (DEC_BATCH, DEC_SEQ, D), f32),
        'cache_a_k': nrm(ks[2], (N_A_LAYERS, DEC_BATCH, la, H, hd), f32),
        'cache_a_v': nrm(ks[3], (N_A_LAYERS, DEC_BATCH, la, H, hd), f32),
        'cache_b_k': nrm(ks[4], (N_B_LAYERS, DEC_BATCH, PAST_LEN, H, hd), f32),
        'cache_b_v': nrm(ks[5], (N_B_LAYERS, DEC_BATCH, PAST_LEN, H, hd), f32),
        'cache_b_logf': jax.nn.log_sigmoid(FORGET_BIAS_INIT + nrm(ks[6], (N_B_LAYERS, DEC_BATCH, PAST_LEN, H), f32)),
        'g_attn': 1.0 + 0.05 * nrm(ks[7], (DEPTH, D), f32),
        'w_qkv': nrm(ks[8], (DEPTH, D, 3 * D), f32) * D ** -0.5,
        'g_q': 1.0 + 0.05 * nrm(ks[9], (DEPTH, hd), f32),
        'g_k': 1.0 + 0.05 * nrm(ks[10], (DEPTH, hd), f32),
        'w_o': nrm(ks[11], (DEPTH, D, D), f32) * D ** -0.5,
        'rel_table': 0.5 * nrm(ks[12], (N_A_LAYERS, H, N_REL), f32),
        'w_f': nrm(ks[13], (N_B_LAYERS, D, H), f32) * D ** -0.5,
        'b_f': FORGET_BIAS_INIT + 0.5 * nrm(ks[14], (N_B_LAYERS, H), f32),
        'g_ffn': 1.0 + 0.05 * nrm(ks[15], (DEPTH, D), f32),
        'w_gate': nrm(ks[16], (DEPTH, D, F), f32) * D ** -0.5,
        'w_up': nrm(ks[17], (DEPTH, D, F), f32) * D ** -0.5,
        'w_down': nrm(ks[18], (DEPTH, F, D), f32) * F ** -0.5,
    }


def reference(x_prompt, x_sample, cache_a_k, cache_a_v, cache_b_k, cache_b_v, cache_b_logf,
              g_attn, w_qkv, g_q, g_k, w_o, rel_table, w_f, b_f, g_ffn, w_gate, w_up, w_down):
    yp, ys = x_prompt, x_sample
    keep = min(WINDOW, x_prompt.shape[1])
    a_kp, a_vp, b_kp, b_vp, b_fp = [], [], [], [], []
    a_ks, a_vs, b_ks, b_vs, b_fs = [], [], [], [], []
    for layer in range(DEPTH):
        hp = rms_norm(yp, g_attn[layer])
        hs = rms_norm(ys, g_attn[layer])
        qp, kp, vp = qkv_heads(hp, w_qkv[layer], g_q[layer], g_k[layer])
        qs, ks, vs = qkv_heads(hs, w_qkv[layer], g_q[layer], g_k[layer])
        if layer % 2 == 0:
            ia = layer // 2
            mp = band_attention_prompt(qp, kp, vp, rel_table[ia])
            ms = band_attention_sample(qs, ks, vs, cache_a_k[ia], cache_a_v[ia], rel_table[ia])
            a_kp.append(kp[:, -keep:])
            a_vp.append(vp[:, -keep:])
            a_ks.append(ks)
            a_vs.append(vs)
        else:
            ib = layer // 2
            lfp = log_forget(hp, w_f[ib], b_f[ib])
            lfs = log_forget(hs, w_f[ib], b_f[ib])
            mp = forgetting_attention_prompt(qp, kp, vp, lfp)
            ms = forgetting_attention_sample(qs, ks, vs, lfs, cache_b_k[ib], cache_b_v[ib], cache_b_logf[ib])
            b_kp.append(kp)
            b_vp.append(vp)
            b_fp.append(lfp)
            b_ks.append(ks)
            b_vs.append(vs)
            b_fs.append(lfs)
        yp = yp + mp @ w_o[layer]
        ys = ys + ms @ w_o[layer]
        yp = yp + swiglu(rms_norm(yp, g_ffn[layer]), w_gate[layer], w_up[layer], w_down[layer])
        ys = ys + swiglu(rms_norm(ys, g_ffn[layer]), w_gate[layer], w_up[layer], w_down[layer])
    return (yp, ys,
            jnp.stack(a_kp), jnp.stack(a_vp), jnp.stack(b_kp), jnp.stack(b_vp), jnp.stack(b_fp),
            jnp.stack(a_ks), jnp.stack(a_vs), jnp.stack(b_ks), jnp.stack(b_vs), jnp.stack(b_fs))
```

```python
import functools

import jax
import jax.numpy as jnp
from jax import lax
from jax.experimental import pallas as pl
from jax.experimental.pallas import tpu as pltpu

N_HEADS = 16
HEAD_DIM = 128
CHUNK = 64
N_LEFT_CHUNKS = 8
WINDOW = N_LEFT_CHUNKS * CHUNK
MAX_REL = 128
EPS = 1e-6
SCALE = HEAD_DIM ** -0.5
NEG = -1e30

BF16 = jnp.bfloat16
F32 = jnp.float32

LANES = 128
HEADS_PER_TILE = 4
ROW_TILE = 512
FFN_TILE = 512
BAND_Q = 256
FOX_BLOCK = 512
SCAN_BLOCK = 512
ROLL_WIDTH = 1024
VMEM_LIMIT = 56 * 1024 * 1024

_NT = (((1,), (1,)), ((), ()))


def _params(*sem):
    return pltpu.CompilerParams(dimension_semantics=sem, vmem_limit_bytes=VMEM_LIMIT)


def _log_sigmoid(z):
    return jnp.minimum(z, 0.0) - jnp.log1p(jnp.exp(-jnp.abs(z)))


def _split3(x):
    hi = x.astype(BF16).astype(F32)
    r1 = x - hi
    mid = r1.astype(BF16).astype(F32)
    lo = r1 - mid
    return jnp.concatenate([hi, mid, lo], axis=0).astype(BF16)


def _qkv_kernel(*refs, forget, sample):
    x_ref, ga_ref, wq_ref, wk_ref, wv_ref, gq_ref, gk_ref = refs[:7]
    rest = list(refs[7:])
    if forget:
        wf_ref, wft_ref, bf_ref, bft_ref = rest[:4]
        rest = rest[4:]
    if sample:
        q_out, k_out, v_out = rest[:3]
        rest = rest[3:]
    else:
        qb_out, kb_out, vb_out, k_out, v_out = rest[:5]
        rest = rest[5:]
    if forget:
        lf_out, lft_out = rest[:2]
        rest = rest[2:]
    (xn_ref,) = rest

    @pl.when(pl.program_id(1) == 0)
    def _():
        x = x_ref[...]
        ms = jnp.mean(x * x, axis=-1, keepdims=True)
        xn = (x * lax.rsqrt(ms + EPS) * ga_ref[...]).astype(BF16)
        xn_ref[...] = xn
        if forget:
            z = jnp.dot(xn, wf_ref[...], preferred_element_type=F32) + bf_ref[...]
            lf_out[...] = _log_sigmoid(z)
            zt = lax.dot_general(wft_ref[...], xn, _NT, preferred_element_type=F32) + bft_ref[...]
            lft_out[...] = _log_sigmoid(zt)

    xn = xn_ref[...]
    q = jnp.dot(xn, wq_ref[...], preferred_element_type=F32)
    k = jnp.dot(xn, wk_ref[...], preferred_element_type=F32)
    v = jnp.dot(xn, wv_ref[...], preferred_element_type=F32)
    gq = gq_ref[...]
    gk = gk_ref[...]
    for hh in range(HEADS_PER_TILE):
        sl = slice(hh * HEAD_DIM, (hh + 1) * HEAD_DIM)
        qh = q[:, sl]
        kh = k[:, sl]
        vh = v[:, sl]
        qn = qh * lax.rsqrt(jnp.mean(qh * qh, axis=-1, keepdims=True) + EPS) * gq * SCALE
        kn = kh * lax.rsqrt(jnp.mean(kh * kh, axis=-1, keepdims=True) + EPS) * gk
        k_out[:, sl] = kn
        v_out[:, sl] = vh
        if sample:
            q_out[:, sl] = qn
        else:
            qb_out[hh] = qn.astype(BF16)
            kb_out[hh] = kn.astype(BF16)
            vb_out[hh] = vh.astype(BF16)


def _qkv_proj(x, g_attn, w_qkv, g_q, g_k, forget_w=None, *, sample, kv_last=None, seq=None):
    m, d = x.shape
    tm = min(ROW_TILE, m)
    tn = HEADS_PER_TILE * HEAD_DIM
    nj = d // tn
    grid = (m // tm, nj)
    forget = forget_w is not None

    in_specs = [
        pl.BlockSpec((tm, d), lambda i, j: (i, 0)),
        pl.BlockSpec((1, d), lambda i, j: (0, 0)),
        pl.BlockSpec((d, tn), lambda i, j: (0, j)),
        pl.BlockSpec((d, tn), lambda i, j: (0, nj + j)),
        pl.BlockSpec((d, tn), lambda i, j: (0, 2 * nj + j)),
        pl.BlockSpec((1, HEAD_DIM), lambda i, j: (0, 0)),
        pl.BlockSpec((1, HEAD_DIM), lambda i, j: (0, 0)),
    ]
    args = [x, g_attn.reshape(1, d), w_qkv, w_qkv, w_qkv,
            g_q.reshape(1, HEAD_DIM), g_k.reshape(1, HEAD_DIM)]
    if forget:
        w_f, b_f = forget_w
        in_specs += [
            pl.BlockSpec((d, N_HEADS), lambda i, j: (0, 0)),
            pl.BlockSpec((N_HEADS, d), lambda i, j: (0, 0)),
            pl.BlockSpec((1, N_HEADS), lambda i, j: (0, 0)),
            pl.BlockSpec((N_HEADS, 1), lambda i, j: (0, 0)),
        ]
        args += [w_f, w_f.T, b_f.reshape(1, N_HEADS), b_f.reshape(N_HEADS, 1)]

    tok_spec = pl.BlockSpec((tm, tn), lambda i, j: (i, j))
    if sample:
        out_shape = [jax.ShapeDtypeStruct((m, d), F32)] * 3
        out_specs = [tok_spec] * 3
    else:
        hm_shape = jax.ShapeDtypeStruct((N_HEADS, m, HEAD_DIM), BF16)
        hm_spec = pl.BlockSpec((HEADS_PER_TILE, tm, HEAD_DIM), lambda i, j: (j, i, 0))
        if kv_last is None:
            kv_shape = jax.ShapeDtypeStruct((m, d), F32)
            kv_spec = tok_spec
        else:
            assert kv_last == tm and seq % tm == 0
            per_seq = seq // tm
            kv_shape = jax.ShapeDtypeStruct((m // seq * kv_last, d), F32)
            kv_spec = pl.BlockSpec(
                (tm, tn),
                lambda i, j: (i // per_seq, jnp.where(i % per_seq == per_seq - 1, j, 0)))
        out_shape = [hm_shape] * 3 + [kv_shape] * 2
        out_specs = [hm_spec] * 3 + [kv_spec] * 2
    if forget:
        out_shape += [jax.ShapeDtypeStruct((m, N_HEADS), F32), jax.ShapeDtypeStruct((N_HEADS, m), F32)]
        out_specs += [pl.BlockSpec((tm, N_HEADS), lambda i, j: (i, 0)),
                      pl.BlockSpec((N_HEADS, tm), lambda i, j: (0, i))]

    return pl.pallas_call(
        functools.partial(_qkv_kernel, forget=forget, sample=sample),
        grid=grid,
        in_specs=in_specs,
        out_specs=out_specs,
        out_shape=out_shape,
        scratch_shapes=[pltpu.VMEM((tm, d), BF16)],
        compiler_params=_params("arbitrary", "arbitrary"),
        name="qkv_sample" if sample else "qkv_prompt",
    )(*args)


def _cumsum_kernel(x_ref, u_ref, o_ref, carry_ref, *, blocks_per_seq):
    @pl.when(pl.program_id(0) % blocks_per_seq == 0)
    def _():
        carry_ref[...] = jnp.zeros_like(carry_ref)

    x = x_ref[...]
    r, t = x.shape
    s = jnp.dot(_split3(x), u_ref[...], preferred_element_type=F32)
    c = s[:r] + s[r:2 * r] + s[2 * r:] + carry_ref[:, :1]
    o_ref[...] = c
    carry_ref[...] = jnp.broadcast_to(c[:, t - 1:t], carry_ref.shape)


def _cumsum_lanes(x, seq):
    r, l = x.shape
    t = min(SCAN_BLOCK, seq)
    upper = jnp.triu(jnp.ones((t, t), BF16))
    return pl.pallas_call(
        functools.partial(_cumsum_kernel, blocks_per_seq=seq // t),
        grid=(l // t,),
        in_specs=[pl.BlockSpec((r, t), lambda i: (0, i)),
                  pl.BlockSpec((t, t), lambda i: (0, 0))],
        out_specs=pl.BlockSpec((r, t), lambda i: (0, i)),
        out_shape=jax.ShapeDtypeStruct((r, l), F32),
        scratch_shapes=[pltpu.VMEM((r, LANES), F32)],
        compiler_params=_params("arbitrary"),
        name="cumsum_lanes",
    )(x, upper)


def _bias_kernel(xp_ref, xs_ref, bp_ref, bs_ref):
    def toeplitz(row, n_rows, lo, hi):
        base = jnp.broadcast_to(row, (8, ROLL_WIDTH))
        blocks = []
        for g in range(n_rows // 8):
            rolled = pltpu.roll(base, 8 * g, 1, stride=1, stride_axis=0)
            blocks.append(rolled[:, lo:hi])
        return jnp.concatenate(blocks, axis=0)

    bp_ref[0] = toeplitz(xp_ref[0], BAND_Q, BAND_Q, ROLL_WIDTH)
    bs_ref[0] = toeplitz(xs_ref[0], bs_ref.shape[1], LANES, LANES + bs_ref.shape[2])


def _rel_bias_blocks(table, n_new):
    assert 2 * BAND_Q == WINDOW and 3 * BAND_Q + BAND_Q == ROLL_WIDTH
    rev = table[:, ::-1]
    n_rel = table.shape[1]
    lead_p = 3 * BAND_Q - MAX_REL
    lead_s = WINDOW + LANES - MAX_REL
    row_p = jnp.pad(rev, ((0, 0), (lead_p, ROLL_WIDTH - n_rel - lead_p)), mode="edge")
    row_s = jnp.pad(rev, ((0, 0), (lead_s, ROLL_WIDTH - n_rel - lead_s)), mode="edge")
    h = table.shape[0]
    ws = WINDOW + LANES
    return pl.pallas_call(
        _bias_kernel,
        grid=(h,),
        in_specs=[pl.BlockSpec((1, 1, ROLL_WIDTH), lambda i: (i, 0, 0)),
                  pl.BlockSpec((1, 1, ROLL_WIDTH), lambda i: (i, 0, 0))],
        out_specs=[pl.BlockSpec((1, BAND_Q, 3 * BAND_Q), lambda i: (i, 0, 0)),
                   pl.BlockSpec((1, n_new, ws), lambda i: (i, 0, 0))],
        out_shape=[jax.ShapeDtypeStruct((h, BAND_Q, 3 * BAND_Q), F32),
                   jax.ShapeDtypeStruct((h, n_new, ws), F32)],
        compiler_params=_params("arbitrary"),
        name="rel_bias",
    )(row_p.reshape(h, 1, ROLL_WIDTH), row_s.reshape(h, 1, ROLL_WIDTH))


def _band_kernel(q_ref, k_ref, v_ref, b_ref, o_ref):
    i = pl.program_id(1)
    q = q_ref[0]
    tq = q.shape[0]
    row_c = lax.broadcasted_iota(jnp.int32, (tq, tq), 0) // CHUNK
    col_c = lax.broadcasted_iota(jnp.int32, (tq, tq), 1) // CHUNK
    valid = [(row_c <= col_c) & (i >= 2), (row_c >= 0) & (i >= 1), col_c <= row_c]
    s_parts, v_parts = [], []
    for t in range(3):
        start = pl.multiple_of(jnp.maximum(i - 2 + t, 0) * tq, tq)
        kb = k_ref[0, pl.ds(start, tq), :]
        v_parts.append(v_ref[0, pl.ds(start, tq), :])
        s = lax.dot_general(q, kb, _NT, preferred_element_type=F32) + b_ref[0, :, t * tq:(t + 1) * tq]
        s_parts.append(jnp.where(valid[t], s, NEG))
    s = jnp.concatenate(s_parts, axis=1)
    m = jnp.max(s, axis=-1, keepdims=True)
    p = jnp.exp(s - m)
    l = jnp.sum(p, axis=-1, keepdims=True)
    pb = p.astype(BF16)
    acc = jnp.dot(pb[:, :tq], v_parts[0], preferred_element_type=F32)
    acc += jnp.dot(pb[:, tq:2 * tq], v_parts[1], preferred_element_type=F32)
    acc += jnp.dot(pb[:, 2 * tq:], v_parts[2], preferred_element_type=F32)
    o_ref[0] = (acc / l).astype(o_ref.dtype)


def _band_attention(qb, kb, vb, bias, batch, seq):
    h, m, hd = qb.shape
    nq = seq // BAND_Q
    return pl.pallas_call(
        _band_kernel,
        grid=(h * batch, nq),
        in_specs=[pl.BlockSpec((1, BAND_Q, hd), lambda g, i: (g // batch, (g % batch) * nq + i, 0)),
                  pl.BlockSpec((1, seq, hd), lambda g, i: (g // batch, g % batch, 0)),
                  pl.BlockSpec((1, seq, hd), lambda g, i: (g // batch, g % batch, 0)),
                  pl.BlockSpec((1, BAND_Q, 3 * BAND_Q), lambda g, i: (g // batch, 0, 0))],
        out_specs=pl.BlockSpec((1, BAND_Q, hd), lambda g, i: (g // batch, (g % batch) * nq + i, 0)),
        out_shape=jax.ShapeDtypeStruct((h, m, hd), BF16),
        compiler_params=_params("arbitrary", "arbitrary"),
        name="band_attention",
    )(qb, kb, vb, bias)


def _band_sample_kernel(q_ref, kn_ref, vn_ref, kc_ref, vc_ref, b_ref, o_ref):
    n = q_ref.shape[0]
    lc = kc_ref.shape[1]
    for h in range(N_HEADS):
        sl = slice(h * HEAD_DIM, (h + 1) * HEAD_DIM)
        q = q_ref[:, sl].astype(BF16)
        s_c = lax.dot_general(q, kc_ref[0, :, sl].astype(BF16), _NT, preferred_element_type=F32)
        s_n = lax.dot_general(q, kn_ref[:, sl].astype(BF16), _NT, preferred_element_type=F32)
        s_c = s_c + b_ref[h, :, :lc]
        s_n = s_n + b_ref[h, :, lc:lc + n]
        m = jnp.maximum(jnp.max(s_c, axis=-1, keepdims=True), jnp.max(s_n, axis=-1, keepdims=True))
        p_c = jnp.exp(s_c - m)
        p_n = jnp.exp(s_n - m)
        l = jnp.sum(p_c, axis=-1, keepdims=True) + jnp.sum(p_n, axis=-1, keepdims=True)
        acc = jnp.dot(p_c.astype(BF16), vc_ref[0, :, sl].astype(BF16), preferred_element_type=F32)
        acc += jnp.dot(p_n.astype(BF16), vn_ref[:, sl].astype(BF16), preferred_element_type=F32)
        o_ref[h] = (acc / l).astype(o_ref.dtype)


def _band_attention_sample(q, k_new, v_new, k_cache, v_cache, bias, n):
    m, d = q.shape
    b, lc, _ = k_cache.shape
    tok = pl.BlockSpec((n, d), lambda i: (i, 0))
    cache = pl.BlockSpec((1, lc, d), lambda i: (i, 0, 0))
    return pl.pallas_call(
        _band_sample_kernel,
        grid=(b,),
        in_specs=[tok, tok, tok, cache, cache,
                  pl.BlockSpec(bias.shape, lambda i: (0, 0, 0))],
        out_specs=pl.BlockSpec((N_HEADS, n, HEAD_DIM), lambda i: (0, i, 0)),
        out_shape=jax.ShapeDtypeStruct((N_HEADS, m, HEAD_DIM), BF16),
        compiler_params=_params("arbitrary"),
        name="band_attention_sample",
    )(q, k_new, v_new, k_cache, v_cache, bias)


def _fox_kernel(q_ref, k_ref, v_ref, c_ref, o_ref):
    i = pl.program_id(1)
    q = q_ref[0]
    tq, hd = q.shape

    def block(j, carry, masked):
        m, l, acc = carry
        start = pl.multiple_of(j * tq, tq)
        s = lax.dot_general(q, k_ref[0, pl.ds(start, tq), :], _NT, preferred_element_type=F32)
        s = s - c_ref[0, pl.ds(j, 1), :]
        if masked:
            row = lax.broadcasted_iota(jnp.int32, (tq, tq), 0)
            col = lax.broadcasted_iota(jnp.int32, (tq, tq), 1)
            s = jnp.where(col <= row, s, NEG)
        m_new = jnp.maximum(m, jnp.max(s, axis=-1, keepdims=True))
        alpha = jnp.exp(m - m_new)
        p = jnp.exp(s - m_new)
        l = alpha * l + jnp.sum(p, axis=-1, keepdims=True)
        pv = jnp.dot(p.astype(BF16), v_ref[0, pl.ds(start, tq), :], preferred_element_type=F32)
        return m_new, l, alpha * acc + pv

    init = (jnp.full((tq, 1), NEG, F32), jnp.zeros((tq, 1), F32), jnp.zeros((tq, hd), F32))
    carry = lax.fori_loop(0, i, lambda j, c: block(j, c, False), init)
    _, l, acc = block(i, carry, True)
    o_ref[0] = (acc / l).astype(o_ref.dtype)


def _fox_attention(qb, kb, vb, c_rows, batch, seq):
    h, m, hd = qb.shape
    t = FOX_BLOCK
    nq = seq // t
    c_blocks = c_rows.reshape(h, m // t, t)
    return pl.pallas_call(
        _fox_kernel,
        grid=(h * batch, nq),
        in_specs=[pl.BlockSpec((1, t, hd), lambda g, i: (g // batch, (g % batch) * nq + i, 0)),
                  pl.BlockSpec((1, seq, hd), lambda g, i: (g // batch, g % batch, 0)),
                  pl.BlockSpec((1, seq, hd), lambda g, i: (g // batch, g % batch, 0)),
                  pl.BlockSpec((1, nq, t), lambda g, i: (g // batch, g % batch, 0))],
        out_specs=pl.BlockSpec((1, t, hd), lambda g, i: (g // batch, (g % batch) * nq + i, 0)),
        out_shape=jax.ShapeDtypeStruct((h, m, hd), BF16),
        compiler_params=_params("arbitrary", "arbitrary"),
        name="fox_attention",
    )(qb, kb, vb, c_blocks)


def _fox_sample_kernel(q_ref, kn_ref, vn_ref, kc_ref, vc_ref, cc_ref, lfn_ref, u_ref, o_ref):
    n = q_ref.shape[0]
    hg = o_ref.shape[0]
    c_cache = cc_ref[0]
    lc = c_cache.shape[1]
    s3 = jnp.dot(_split3(lfn_ref[0]), u_ref[...], preferred_element_type=F32)
    c_new = s3[:hg] + s3[hg:2 * hg] + s3[2 * hg:] + c_cache[:, lc - 1:lc]
    row = lax.broadcasted_iota(jnp.int32, (n, n), 0)
    col = lax.broadcasted_iota(jnp.int32, (n, n), 1)
    for h in range(hg):
        sl = slice(h * HEAD_DIM, (h + 1) * HEAD_DIM)
        q = q_ref[:, sl].astype(BF16)
        s_c = lax.dot_general(q, kc_ref[0, :, sl].astype(BF16), _NT, preferred_element_type=F32)
        s_n = lax.dot_general(q, kn_ref[:, sl].astype(BF16), _NT, preferred_element_type=F32)
        s_c = s_c - c_cache[h:h + 1, :]
        s_n = jnp.where(col <= row, s_n - c_new[h:h + 1, :], NEG)
        m = jnp.maximum(jnp.max(s_c, axis=-1, keepdims=True), jnp.max(s_n, axis=-1, keepdims=True))
        p_c = jnp.exp(s_c - m)
        p_n = jnp.exp(s_n - m)
        l = jnp.sum(p_c, axis=-1, keepdims=True) + jnp.sum(p_n, axis=-1, keepdims=True)
        acc = jnp.dot(p_c.astype(BF16), vc_ref[0, :, sl].astype(BF16), preferred_element_type=F32)
        acc += jnp.dot(p_n.astype(BF16), vn_ref[:, sl].astype(BF16), preferred_element_type=F32)
        o_ref[h] = (acc / l).astype(o_ref.dtype)


def _fox_attention_sample(q, k_new, v_new, k_cache, v_cache, c_cache, lf_new, n):
    m, d = q.shape
    b, lc, _ = k_cache.shape
    hg = N_HEADS // 2
    dg = hg * HEAD_DIM
    tok = pl.BlockSpec((n, dg), lambda i, g: (i, g))
    cache = pl.BlockSpec((1, lc, dg), lambda i, g: (i, 0, g))
    upper = jnp.triu(jnp.ones((n, n), BF16))
    return pl.pallas_call(
        _fox_sample_kernel,
        grid=(b, 2),
        in_specs=[tok, tok, tok, cache, cache,
                  pl.BlockSpec((1, hg, lc), lambda i, g: (i, g, 0)),
                  pl.BlockSpec((1, hg, n), lambda i, g: (i, g, 0)),
                  pl.BlockSpec((n, n), lambda i, g: (0, 0))],
        out_specs=pl.BlockSpec((hg, n, HEAD_DIM), lambda i, g: (g, i, 0)),
        out_shape=jax.ShapeDtypeStruct((N_HEADS, m, HEAD_DIM), BF16),
        compiler_params=_params("arbitrary", "arbitrary"),
        name="fox_attention_sample",
    )(q, k_new, v_new, k_cache, v_cache, c_cache, lf_new, upper)


def _out_proj_kernel(a_ref, w_ref, x_ref, o_ref):
    a = jnp.concatenate([a_ref[h] for h in range(N_HEADS)], axis=1)
    o_ref[...] = x_ref[...] + jnp.dot(a, w_ref[...], preferred_element_type=F32)


def _out_proj(attn, w_o, x):
    m, d = x.shape
    tm = min(ROW_TILE, m)
    return pl.pallas_call(
        _out_proj_kernel,
        grid=(m // tm,),
        in_specs=[pl.BlockSpec((N_HEADS, tm, HEAD_DIM), lambda i: (0, i, 0)),
                  pl.BlockSpec((d, d), lambda i: (0, 0)),
                  pl.BlockSpec((tm, d), lambda i: (i, 0))],
        out_specs=pl.BlockSpec((tm, d), lambda i: (i, 0)),
        out_shape=jax.ShapeDtypeStruct((m, d), F32),
        compiler_params=_params("arbitrary"),
        name="out_proj",
    )(attn, w_o, x)


def _ffn_kernel(x_ref, g_ref, wg_ref, wu_ref, wd_ref, o_ref, xn_ref):
    @pl.when(pl.program_id(1) == 0)
    def _():
        x = x_ref[...]
        ms = jnp.mean(x * x, axis=-1, keepdims=True)
        xn_ref[...] = (x * lax.rsqrt(ms + EPS) * g_ref[...]).astype(BF16)
        o_ref[...] = x

    xn = xn_ref[...]
    gate = jnp.dot(xn, wg_ref[...], preferred_element_type=F32)
    up = jnp.dot(xn, wu_ref[...], preferred_element_type=F32)
    hidden = (gate * jax.nn.sigmoid(gate) * up).astype(BF16)
    o_ref[...] += jnp.dot(hidden, wd_ref[...], preferred_element_type=F32)


def _ffn(x, g_ffn, w_gate, w_up, w_down):
    m, d = x.shape
    f = w_gate.shape[1]
    tm = min(ROW_TILE, m)
    tf = FFN_TILE
    return pl.pallas_call(
        _ffn_kernel,
        grid=(m // tm, f // tf),
        in_specs=[pl.BlockSpec((tm, d), lambda i, j: (i, 0)),
                  pl.BlockSpec((1, d), lambda i, j: (0, 0)),
                  pl.BlockSpec((d, tf), lambda i, j: (0, j)),
                  pl.BlockSpec((d, tf), lambda i, j: (0, j)),
                  pl.BlockSpec((tf, d), lambda i, j: (j, 0))],
        out_specs=pl.BlockSpec((tm, d), lambda i, j: (i, 0)),
        out_shape=jax.ShapeDtypeStruct((m, d), F32),
        scratch_shapes=[pltpu.VMEM((tm, d), BF16)],
        compiler_params=_params("arbitrary", "arbitrary"),
        name="ffn",
    )(x, g_ffn.reshape(1, d), w_gate, w_up, w_down)


def kernel(x_prompt, x_sample, cache_a_k, cache_a_v, cache_b_k, cache_b_v, cache_b_logf,
           g_attn, w_qkv, g_q, g_k, w_o, rel_table, w_f, b_f, g_ffn, w_gate, w_up, w_down):
    batch, seq, d = x_prompt.shape
    dec_batch, n_new, _ = x_sample.shape
    depth = g_attn.shape[0]
    keep = min(WINDOW, seq)
    h, hd = N_HEADS, HEAD_DIM
    assert d == h * hd and seq % FOX_BLOCK == 0 and keep == WINDOW

    w_qkv_b = w_qkv.astype(BF16)
    w_o_b = w_o.astype(BF16)
    w_f_b = w_f.astype(BF16)
    w_gate_b = w_gate.astype(BF16)
    w_up_b = w_up.astype(BF16)
    w_down_b = w_down.astype(BF16)

    yp = x_prompt.reshape(batch * seq, d)
    ys = x_sample.reshape(dec_batch * n_new, d)
    a_kp, a_vp, b_kp, b_vp, b_fp = [], [], [], [], []
    a_ks, a_vs, b_ks, b_vs, b_fs = [], [], [], [], []

    for layer in range(depth):
        idx = layer // 2
        if layer % 2 == 0:
            qb, kb, vb, k_last, v_last = _qkv_proj(
                yp, g_attn[layer], w_qkv_b[layer], g_q[layer], g_k[layer],
                sample=False, kv_last=keep, seq=seq)
            qs, ks, vs = _qkv_proj(ys, g_attn[layer], w_qkv_b[layer], g_q[layer], g_k[layer], sample=True)
            bias_p, bias_s = _rel_bias_blocks(rel_table[idx], n_new)
            mp = _band_attention(qb, kb, vb, bias_p, batch, seq)
            la = cache_a_k.shape[2]
            ms = _band_attention_sample(qs, ks, vs, cache_a_k[idx].reshape(dec_batch, la, d),
                                        cache_a_v[idx].reshape(dec_batch, la, d), bias_s, n_new)
            a_kp.append(k_last.reshape(batch, keep, h, hd))
            a_vp.append(v_last.reshape(batch, keep, h, hd))
            a_ks.append(ks.reshape(dec_batch, n_new, h, hd))
            a_vs.append(vs.reshape(dec_batch, n_new, h, hd))
        else:
            fw = (w_f_b[idx], b_f[idx])
            qb, kb, vb, k_all, v_all, lf, lft = _qkv_proj(
                yp, g_attn[layer], w_qkv_b[layer], g_q[layer], g_k[layer], fw, sample=False)
            qs, ks, vs, lfs, lfts = _qkv_proj(
                ys, g_attn[layer], w_qkv_b[layer], g_q[layer], g_k[layer], fw, sample=True)
            mp = _fox_attention(qb, kb, vb, _cumsum_lanes(lft, seq), batch, seq)
            past = cache_b_k.shape[2]
            lf_cache = cache_b_logf[idx].transpose(0, 2, 1).reshape(dec_batch * h, past)
            c_cache = _cumsum_lanes(lf_cache, past).reshape(dec_batch, h, past)
            lf_new = lfts.reshape(h, dec_batch, n_new).transpose(1, 0, 2)
            ms = _fox_attention_sample(qs, ks, vs, cache_b_k[idx].reshape(dec_batch, past, d),
                                       cache_b_v[idx].reshape(dec_batch, past, d), c_cache, lf_new, n_new)
            b_kp.append(k_all.reshape(batch, seq, h, hd))
            b_vp.append(v_all.reshape(batch, seq, h, hd))
            b_fp.append(lf.reshape(batch, seq, h))
            b_ks.append(ks.reshape(dec_batch, n_new, h, hd))
            b_vs.append(vs.reshape(dec_batch, n_new, h, hd))
            b_fs.append(lfs.reshape(dec_batch, n_new, h))
        yp = _out_proj(mp, w_o_b[layer], yp)
        ys = _out_proj(ms, w_o_b[layer], ys)
        yp = _ffn(yp, g_ffn[layer], w_gate_b[layer], w_up_b[layer], w_down_b[layer])
        ys = _ffn(ys, g_ffn[layer], w_gate_b[layer], w_up_b[layer], w_down_b[layer])

    return (yp.reshape(batch, seq, d), ys.reshape(dec_batch, n_new, d),
            jnp.stack(a_kp), jnp.stack(a_vp), jnp.stack(b_kp), jnp.stack(b_vp), jnp.stack(b_fp),
            jnp.stack(a_ks), jnp.stack(a_vs), jnp.stack(b_ks), jnp.stack(b_vs), jnp.stack(b_fs))
```

```python
import functools
import math

import jax
import jax.numpy as jnp
from jax import lax
from jax.experimental import pallas as pl
from jax.experimental.pallas import tpu as pltpu

N_HEADS = 16
HEAD_DIM = 128
CHUNK = 64
N_LEFT_CHUNKS = 8
WINDOW = N_LEFT_CHUNKS * CHUNK
MAX_REL = 128
EPS = 1e-6
SCALE = HEAD_DIM ** -0.5
LOG2E = math.log2(math.e)
NEG = -1e30

BF16 = jnp.bfloat16
F32 = jnp.float32

LANES = 128
HEADS_PER_TILE = 4
ROW_TILE = 512
FFN_TILE = 512
BAND_Q = 256
BAND_STEP = 1024
FOX_STEP = 1024
SCAN_BLOCK = 512
ROLL_WIDTH = 1024
VMEM_LIMIT = 56 * 1024 * 1024

_NT = (((1,), (1,)), ((), ()))


def _params(*sem):
    return pltpu.CompilerParams(dimension_semantics=sem, vmem_limit_bytes=VMEM_LIMIT)


def _log_sigmoid(z):
    return jnp.minimum(z, 0.0) - jnp.log1p(jnp.exp(-jnp.abs(z)))


def _split3(x):
    hi = x.astype(BF16).astype(F32)
    r1 = x - hi
    mid = r1.astype(BF16).astype(F32)
    lo = r1 - mid
    return jnp.concatenate([hi, mid, lo], axis=0).astype(BF16)


def _head_rows(head, n_rows):
    return pl.ds(head, n_rows, stride=N_HEADS)


def _qkv_kernel(*refs, forget, sample, q_scale, n_alias):
    x_ref, ga_ref, wq_ref, wk_ref, wv_ref, gq_ref, gk_ref = refs[:7]
    rest = list(refs[7:])
    if forget:
        wf_ref, bf_ref = rest[:2]
        rest = rest[2:]
    rest = rest[n_alias:]
    if sample:
        q_out, k_out, v_out = rest[:3]
        rest = rest[3:]
    else:
        qb_out, kb_out, vb_out, k_out, v_out = rest[:5]
        rest = rest[5:]
    if forget:
        lf_out = rest[0]
        rest = rest[1:]
    (xn_ref,) = rest
    j = pl.program_id(1)
    tm = x_ref.shape[0]

    @pl.when(j == 0)
    def _():
        x = x_ref[...]
        ms = jnp.mean(x * x, axis=-1, keepdims=True)
        xn = (x * lax.rsqrt(ms + EPS) * ga_ref[...]).astype(BF16)
        xn_ref[...] = xn
        if forget:
            z = jnp.dot(xn, wf_ref[...], preferred_element_type=F32) + bf_ref[...]
            lf_out[...] = _log_sigmoid(z)

    xn = xn_ref[...]
    q = jnp.dot(xn, wq_ref[...], preferred_element_type=F32)
    k = jnp.dot(xn, wk_ref[...], preferred_element_type=F32)
    v = jnp.dot(xn, wv_ref[...], preferred_element_type=F32)
    gq = gq_ref[...]
    gk = gk_ref[...]
    for hh in range(HEADS_PER_TILE):
        sl = slice(hh * HEAD_DIM, (hh + 1) * HEAD_DIM)
        qh = q[:, sl]
        kh = k[:, sl]
        vh = v[:, sl]
        qn = qh * lax.rsqrt(jnp.mean(qh * qh, axis=-1, keepdims=True) + EPS) * gq * q_scale
        kn = kh * lax.rsqrt(jnp.mean(kh * kh, axis=-1, keepdims=True) + EPS) * gk
        if sample:
            q_out[:, sl] = qn
            k_out[:, sl] = kn
            v_out[:, sl] = vh
        else:
            qb_out[hh] = qn.astype(BF16)
            kb_out[hh] = kn.astype(BF16)
            vb_out[hh] = vh.astype(BF16)
            rows = _head_rows(j * HEADS_PER_TILE + hh, tm)
            k_out[rows, :] = kn
            v_out[rows, :] = vh


def _qkv_proj(x, g_attn, w_qkv, layer, g_q, g_k, forget_w=None, *, sample, q_scale=SCALE,
              kv_last=None, seq=None, kv_slots=None, kv_slot=0, kv_prev=None):
    m, d = x.shape
    tm = min(ROW_TILE, m)
    tn = HEADS_PER_TILE * HEAD_DIM
    nj = d // tn
    grid = (m // tm, nj)
    forget = forget_w is not None

    in_specs = [
        pl.BlockSpec((tm, d), lambda i, j: (i, 0)),
        pl.BlockSpec((1, d), lambda i, j: (0, 0)),
        pl.BlockSpec((None, d, tn), lambda i, j: (layer, 0, j)),
        pl.BlockSpec((None, d, tn), lambda i, j: (layer, 0, nj + j)),
        pl.BlockSpec((None, d, tn), lambda i, j: (layer, 0, 2 * nj + j)),
        pl.BlockSpec((1, HEAD_DIM), lambda i, j: (0, 0)),
        pl.BlockSpec((1, HEAD_DIM), lambda i, j: (0, 0)),
    ]
    args = [x, g_attn.reshape(1, d), w_qkv, w_qkv, w_qkv,
            g_q.reshape(1, HEAD_DIM), g_k.reshape(1, HEAD_DIM)]
    if forget:
        w_f, b_f = forget_w
        in_specs += [pl.BlockSpec((d, N_HEADS), lambda i, j: (0, 0)),
                     pl.BlockSpec((1, N_HEADS), lambda i, j: (0, 0))]
        args += [w_f, b_f.reshape(1, N_HEADS)]

    aliases = {}
    if sample:
        tok_spec = pl.BlockSpec((tm, tn), lambda i, j: (i, j))
        out_shape = [jax.ShapeDtypeStruct((m, d), F32)] * 3
        out_specs = [tok_spec] * 3
    else:
        hm_shape = jax.ShapeDtypeStruct((N_HEADS, m, HEAD_DIM), BF16)
        hm_spec = pl.BlockSpec((HEADS_PER_TILE, tm, HEAD_DIM), lambda i, j: (j, i, 0))
        if kv_last is None:
            kv_rows = m
            kv_spec = pl.BlockSpec((None, tm * N_HEADS, HEAD_DIM), lambda i, j: (kv_slot, i, 0))
        else:
            assert kv_last == tm and seq % tm == 0
            per_seq = seq // tm
            kv_rows = m // seq * kv_last
            kv_spec = pl.BlockSpec((None, tm * N_HEADS, HEAD_DIM),
                                   lambda i, j: (kv_slot, i // per_seq, 0))
        kv_shape = jax.ShapeDtypeStruct((kv_slots, kv_rows * N_HEADS, HEAD_DIM), F32)
        out_shape = [hm_shape] * 3 + [kv_shape] * 2
        out_specs = [hm_spec] * 3 + [kv_spec] * 2
        if kv_prev is not None:
            aliases = {len(args): 3, len(args) + 1: 4}
            in_specs += [pl.BlockSpec(memory_space=pl.ANY)] * 2
            args += list(kv_prev)
    if forget:
        out_shape += [jax.ShapeDtypeStruct((m, N_HEADS), F32)]
        out_specs += [pl.BlockSpec((tm, N_HEADS), lambda i, j: (i, 0))]

    return pl.pallas_call(
        functools.partial(_qkv_kernel, forget=forget, sample=sample, q_scale=q_scale,
                          n_alias=len(aliases)),
        grid=grid,
        in_specs=in_specs,
        out_specs=out_specs,
        out_shape=out_shape,
        scratch_shapes=[pltpu.VMEM((tm, d), BF16)],
        input_output_aliases=aliases,
        compiler_params=_params("arbitrary", "arbitrary"),
        name="qkv_sample" if sample else "qkv_prompt",
    )(*args)


def _cumsum_kernel(x_ref, u_ref, o_ref, carry_ref, *, blocks_per_seq):
    @pl.when(pl.program_id(0) % blocks_per_seq == 0)
    def _():
        carry_ref[...] = jnp.zeros_like(carry_ref)

    x = x_ref[...]
    r, t = x.shape
    s = jnp.dot(_split3(x), u_ref[...], preferred_element_type=F32)
    c = s[:r] + s[r:2 * r] + s[2 * r:] + carry_ref[:, :1]
    o_ref[...] = c
    carry_ref[...] = jnp.broadcast_to(c[:, t - 1:t], carry_ref.shape)


def _cumsum_lanes(x, seq):
    r, l = x.shape
    t = min(SCAN_BLOCK, seq)
    upper = jnp.triu(jnp.ones((t, t), BF16))
    return pl.pallas_call(
        functools.partial(_cumsum_kernel, blocks_per_seq=seq // t),
        grid=(l // t,),
        in_specs=[pl.BlockSpec((r, t), lambda i: (0, i)),
                  pl.BlockSpec((t, t), lambda i: (0, 0))],
        out_specs=pl.BlockSpec((r, t), lambda i: (0, i)),
        out_shape=jax.ShapeDtypeStruct((r, l), F32),
        scratch_shapes=[pltpu.VMEM((r, LANES), F32)],
        compiler_params=_params("arbitrary"),
        name="cumsum_lanes",
    )(x, upper)


def _bias_kernel(xp_ref, xs_ref, bp_ref, bs_ref):
    def toeplitz(row, n_rows, lo, hi):
        base = jnp.broadcast_to(row, (8, ROLL_WIDTH))
        blocks = []
        for g in range(n_rows // 8):
            rolled = pltpu.roll(base, 8 * g, 1, stride=1, stride_axis=0)
            blocks.append(rolled[:, lo:hi])
        return jnp.concatenate(blocks, axis=0)

    bp_ref[0] = toeplitz(xp_ref[0], BAND_Q, BAND_Q, ROLL_WIDTH)
    bs_ref[0] = toeplitz(xs_ref[0], bs_ref.shape[1], LANES, LANES + bs_ref.shape[2])


def _rel_bias_blocks(table, n_new):
    assert 2 * BAND_Q == WINDOW and 3 * BAND_Q + BAND_Q == ROLL_WIDTH
    rev = table[:, ::-1]
    n_rel = table.shape[1]
    lead_p = 3 * BAND_Q - MAX_REL
    lead_s = WINDOW + LANES - MAX_REL
    row_p = jnp.pad(rev, ((0, 0), (lead_p, ROLL_WIDTH - n_rel - lead_p)), mode="edge")
    row_s = jnp.pad(rev, ((0, 0), (lead_s, ROLL_WIDTH - n_rel - lead_s)), mode="edge")
    h = table.shape[0]
    ws = WINDOW + LANES
    return pl.pallas_call(
        _bias_kernel,
        grid=(h,),
        in_specs=[pl.BlockSpec((1, 1, ROLL_WIDTH), lambda i: (i, 0, 0)),
                  pl.BlockSpec((1, 1, ROLL_WIDTH), lambda i: (i, 0, 0))],
        out_specs=[pl.BlockSpec((1, BAND_Q, 3 * BAND_Q), lambda i: (i, 0, 0)),
                   pl.BlockSpec((1, n_new, ws), lambda i: (i, 0, 0))],
        out_shape=[jax.ShapeDtypeStruct((h, BAND_Q, 3 * BAND_Q), F32),
                   jax.ShapeDtypeStruct((h, n_new, ws), F32)],
        compiler_params=_params("arbitrary"),
        name="rel_bias",
    )(row_p.reshape(h, 1, ROLL_WIDTH), row_s.reshape(h, 1, ROLL_WIDTH))


def _band_kernel(q_ref, k_ref, v_ref, b_ref, o_ref):
    tq = BAND_Q
    n_sub = q_ref.shape[1] // tq
    row_c = lax.broadcasted_iota(jnp.int32, (tq, tq), 0) // CHUNK
    col_c = lax.broadcasted_iota(jnp.int32, (tq, tq), 1) // CHUNK
    for sub in range(n_sub):
        i = pl.program_id(1) * n_sub + sub
        q = q_ref[0, sub * tq:(sub + 1) * tq, :]
        valid = [(row_c <= col_c) & (i >= 2), (row_c >= 0) & (i >= 1), col_c <= row_c]
        s_parts, v_parts = [], []
        for t in range(3):
            start = pl.multiple_of(jnp.maximum(i - 2 + t, 0) * tq, tq)
            kb = k_ref[0, pl.ds(start, tq), :]
            v_parts.append(v_ref[0, pl.ds(start, tq), :])
            s = lax.dot_general(q, kb, _NT, preferred_element_type=F32)
            s = s + b_ref[0, :, t * tq:(t + 1) * tq]
            s_parts.append(jnp.where(valid[t], s, NEG))
        s = jnp.concatenate(s_parts, axis=1)
        m = jnp.max(s, axis=-1, keepdims=True)
        p = jnp.exp(s - m)
        l = jnp.sum(p, axis=-1, keepdims=True)
        pb = p.astype(BF16)
        acc = jnp.dot(pb[:, :tq], v_parts[0], preferred_element_type=F32)
        acc += jnp.dot(pb[:, tq:2 * tq], v_parts[1], preferred_element_type=F32)
        acc += jnp.dot(pb[:, 2 * tq:], v_parts[2], preferred_element_type=F32)
        o_ref[0, sub * tq:(sub + 1) * tq, :] = (acc / l).astype(o_ref.dtype)


def _band_attention(qb, kb, vb, bias, batch, seq):
    h, m, hd = qb.shape
    step = min(BAND_STEP, seq)
    nq = seq // step
    return pl.pallas_call(
        _band_kernel,
        grid=(h * batch, nq),
        in_specs=[pl.BlockSpec((1, step, hd), lambda g, i: (g // batch, (g % batch) * nq + i, 0)),
                  pl.BlockSpec((1, seq, hd), lambda g, i: (g // batch, g % batch, 0)),
                  pl.BlockSpec((1, seq, hd), lambda g, i: (g // batch, g % batch, 0)),
                  pl.BlockSpec((1, BAND_Q, 3 * BAND_Q), lambda g, i: (g // batch, 0, 0))],
        out_specs=pl.BlockSpec((1, step, hd), lambda g, i: (g // batch, (g % batch) * nq + i, 0)),
        out_shape=jax.ShapeDtypeStruct((h, m, hd), BF16),
        compiler_params=_params("arbitrary", "arbitrary"),
        name="band_attention",
    )(qb, kb, vb, bias)


def _band_sample_kernel(q_ref, kn_ref, vn_ref, kc_ref, vc_ref, b_ref, o_ref):
    n = q_ref.shape[0]
    lc = kc_ref.shape[0] // N_HEADS
    for h in range(N_HEADS):
        sl = slice(h * HEAD_DIM, (h + 1) * HEAD_DIM)
        rows = _head_rows(h, lc)
        q = q_ref[:, sl].astype(BF16)
        s_c = lax.dot_general(q, kc_ref[rows, :].astype(BF16), _NT, preferred_element_type=F32)
        s_n = lax.dot_general(q, kn_ref[:, sl].astype(BF16), _NT, preferred_element_type=F32)
        s_c = s_c + b_ref[h, :, :lc]
        s_n = s_n + b_ref[h, :, lc:lc + n]
        m = jnp.maximum(jnp.max(s_c, axis=-1, keepdims=True), jnp.max(s_n, axis=-1, keepdims=True))
        p_c = jnp.exp(s_c - m)
        p_n = jnp.exp(s_n - m)
        l = jnp.sum(p_c, axis=-1, keepdims=True) + jnp.sum(p_n, axis=-1, keepdims=True)
        acc = jnp.dot(p_c.astype(BF16), vc_ref[rows, :].astype(BF16), preferred_element_type=F32)
        acc += jnp.dot(p_n.astype(BF16), vn_ref[:, sl].astype(BF16), preferred_element_type=F32)
        o_ref[h] = (acc / l).astype(o_ref.dtype)


def _band_attention_sample(q, k_new, v_new, k_cache, v_cache, idx, bias, n):
    m, d = q.shape
    _, b, rows, hd = k_cache.shape
    tok = pl.BlockSpec((n, d), lambda i: (i, 0))
    cache = pl.BlockSpec((None, None, rows, hd), lambda i: (idx, i, 0, 0))
    return pl.pallas_call(
        _band_sample_kernel,
        grid=(b,),
        in_specs=[tok, tok, tok, cache, cache,
                  pl.BlockSpec(bias.shape, lambda i: (0, 0, 0))],
        out_specs=pl.BlockSpec((N_HEADS, n, HEAD_DIM), lambda i: (0, i, 0)),
        out_shape=jax.ShapeDtypeStruct((N_HEADS, m, HEAD_DIM), BF16),
        compiler_params=_params("arbitrary"),
        name="band_attention_sample",
    )(q, k_new, v_new, k_cache, v_cache, bias)


def _online_softmax_step(s, carry, v):
    m, l, acc = carry
    m_new = jnp.maximum(m, jnp.max(s, axis=-1, keepdims=True))
    alpha = jnp.exp2(m - m_new)
    p = jnp.exp2(s - m_new)
    l = alpha * l + jnp.sum(p, axis=-1, keepdims=True)
    acc = alpha * acc + jnp.dot(p.astype(BF16), v, preferred_element_type=F32)
    return m_new, l, acc


def _fox_kernel(q_ref, k_ref, v_ref, c_ref, o_ref):
    i = pl.program_id(1)
    tq, hd = q_ref.shape[1], q_ref.shape[2]
    half = tq // 2
    q0 = q_ref[0, :half, :]
    q1 = q_ref[0, half:, :]

    def full_chunk(j, carry):
        start = pl.multiple_of(j * tq, tq)
        kj = k_ref[0, pl.ds(start, tq), :]
        vj = v_ref[0, pl.ds(start, tq), :]
        cj = c_ref[0, pl.ds(j, 1), :] * LOG2E
        s0 = lax.dot_general(q0, kj, _NT, preferred_element_type=F32) - cj
        s1 = lax.dot_general(q1, kj, _NT, preferred_element_type=F32) - cj
        return _online_softmax_step(s0, carry[0], vj), _online_softmax_step(s1, carry[1], vj)

    init = (jnp.full((half, 1), NEG, F32), jnp.zeros((half, 1), F32), jnp.zeros((half, hd), F32))
    c0, c1 = lax.fori_loop(0, i, full_chunk, (init, init))

    start = pl.multiple_of(i * tq, tq)
    ka = k_ref[0, pl.ds(start, half), :]
    va = v_ref[0, pl.ds(start, half), :]
    kb = k_ref[0, pl.ds(start + half, half), :]
    vb = v_ref[0, pl.ds(start + half, half), :]
    cj = c_ref[0, pl.ds(i, 1), :] * LOG2E
    ca, cb = cj[:, :half], cj[:, half:]
    causal = (lax.broadcasted_iota(jnp.int32, (half, half), 1)
              <= lax.broadcasted_iota(jnp.int32, (half, half), 0))
    s00 = lax.dot_general(q0, ka, _NT, preferred_element_type=F32) - ca
    c0 = _online_softmax_step(jnp.where(causal, s00, NEG), c0, va)
    s10 = lax.dot_general(q1, ka, _NT, preferred_element_type=F32) - ca
    c1 = _online_softmax_step(s10, c1, va)
    s11 = lax.dot_general(q1, kb, _NT, preferred_element_type=F32) - cb
    c1 = _online_softmax_step(jnp.where(causal, s11, NEG), c1, vb)
    o_ref[0, :half, :] = (c0[2] / c0[1]).astype(o_ref.dtype)
    o_ref[0, half:, :] = (c1[2] / c1[1]).astype(o_ref.dtype)


def _fox_attention(qb, kb, vb, c_rows, batch, seq):
    h, m, hd = qb.shape
    t = min(FOX_STEP, seq)
    nq = seq // t
    c_blocks = c_rows.reshape(h, m // t, t)
    return pl.pallas_call(
        _fox_kernel,
        grid=(h * batch, nq),
        in_specs=[pl.BlockSpec((1, t, hd), lambda g, i: (g // batch, (g % batch) * nq + i, 0)),
                  pl.BlockSpec((1, seq, hd), lambda g, i: (g // batch, g % batch, 0)),
                  pl.BlockSpec((1, seq, hd), lambda g, i: (g // batch, g % batch, 0)),
                  pl.BlockSpec((1, nq, t), lambda g, i: (g // batch, g % batch, 0))],
        out_specs=pl.BlockSpec((1, t, hd), lambda g, i: (g // batch, (g % batch) * nq + i, 0)),
        out_shape=jax.ShapeDtypeStruct((h, m, hd), BF16),
        compiler_params=_params("arbitrary", "arbitrary"),
        name="fox_attention",
    )(qb, kb, vb, c_blocks)


def _fox_sample_kernel(q_ref, kn_ref, vn_ref, kc_ref, vc_ref, cc_ref, lfn_ref, u_ref, o_ref):
    n = q_ref.shape[0]
    c_cache = cc_ref[0]
    lc = c_cache.shape[1]
    s3 = jnp.dot(_split3(lfn_ref[0]), u_ref[...], preferred_element_type=F32)
    c_new = s3[:N_HEADS] + s3[N_HEADS:2 * N_HEADS] + s3[2 * N_HEADS:] + c_cache[:, lc - 1:lc]
    row = lax.broadcasted_iota(jnp.int32, (n, n), 0)
    col = lax.broadcasted_iota(jnp.int32, (n, n), 1)
    for h in range(N_HEADS):
        sl = slice(h * HEAD_DIM, (h + 1) * HEAD_DIM)
        rows = _head_rows(h, lc)
        q = q_ref[:, sl].astype(BF16)
        s_c = lax.dot_general(q, kc_ref[rows, :].astype(BF16), _NT, preferred_element_type=F32)
        s_n = lax.dot_general(q, kn_ref[:, sl].astype(BF16), _NT, preferred_element_type=F32)
        s_c = s_c - c_cache[h:h + 1, :]
        s_n = jnp.where(col <= row, s_n - c_new[h:h + 1, :], NEG)
        m = jnp.maximum(jnp.max(s_c, axis=-1, keepdims=True), jnp.max(s_n, axis=-1, keepdims=True))
        p_c = jnp.exp(s_c - m)
        p_n = jnp.exp(s_n - m)
        l = jnp.sum(p_c, axis=-1, keepdims=True) + jnp.sum(p_n, axis=-1, keepdims=True)
        acc = jnp.dot(p_c.astype(BF16), vc_ref[rows, :].astype(BF16), preferred_element_type=F32)
        acc += jnp.dot(p_n.astype(BF16), vn_ref[:, sl].astype(BF16), preferred_element_type=F32)
        o_ref[h] = (acc / l).astype(o_ref.dtype)


def _fox_attention_sample(q, k_new, v_new, k_cache, v_cache, idx, c_cache, lf_new, n):
    m, d = q.shape
    _, b, rows, hd = k_cache.shape
    lc = rows // N_HEADS
    tok = pl.BlockSpec((n, d), lambda i: (i, 0))
    cache = pl.BlockSpec((None, None, rows, hd), lambda i: (idx, i, 0, 0))
    upper = jnp.triu(jnp.ones((n, n), BF16))
    return pl.pallas_call(
        _fox_sample_kernel,
        grid=(b,),
        in_specs=[tok, tok, tok, cache, cache,
                  pl.BlockSpec((1, N_HEADS, lc), lambda i: (i, 0, 0)),
                  pl.BlockSpec((1, N_HEADS, n), lambda i: (i, 0, 0)),
                  pl.BlockSpec((n, n), lambda i: (0, 0))],
        out_specs=pl.BlockSpec((N_HEADS, n, HEAD_DIM), lambda i: (0, i, 0)),
        out_shape=jax.ShapeDtypeStruct((N_HEADS, m, HEAD_DIM), BF16),
        compiler_params=_params("arbitrary"),
        name="fox_attention_sample",
    )(q, k_new, v_new, k_cache, v_cache, c_cache, lf_new, upper)


def _out_proj_kernel(a_ref, w_ref, x_ref, o_ref):
    a = jnp.concatenate([a_ref[h] for h in range(N_HEADS)], axis=1)
    o_ref[...] = x_ref[...] + jnp.dot(a, w_ref[...], preferred_element_type=F32)


def _out_proj(attn, w_o, layer, x):
    m, d = x.shape
    tm = min(ROW_TILE, m)
    return pl.pallas_call(
        _out_proj_kernel,
        grid=(m // tm,),
        in_specs=[pl.BlockSpec((N_HEADS, tm, HEAD_DIM), lambda i: (0, i, 0)),
                  pl.BlockSpec((None, d, d), lambda i: (layer, 0, 0)),
                  pl.BlockSpec((tm, d), lambda i: (i, 0))],
        out_specs=pl.BlockSpec((tm, d), lambda i: (i, 0)),
        out_shape=jax.ShapeDtypeStruct((m, d), F32),
        compiler_params=_params("arbitrary"),
        name="out_proj",
    )(attn, w_o, x)


def _ffn_kernel(x_ref, g_ref, wg_ref, wu_ref, wd_ref, o_ref, xn_ref):
    @pl.when(pl.program_id(1) == 0)
    def _():
        x = x_ref[...]
        ms = jnp.mean(x * x, axis=-1, keepdims=True)
        xn_ref[...] = (x * lax.rsqrt(ms + EPS) * g_ref[...]).astype(BF16)
        o_ref[...] = x

    xn = xn_ref[...]
    gate = jnp.dot(xn, wg_ref[...], preferred_element_type=F32)
    up = jnp.dot(xn, wu_ref[...], preferred_element_type=F32)
    hidden = (gate * jax.nn.sigmoid(gate) * up).astype(BF16)
    o_ref[...] += jnp.dot(hidden, wd_ref[...], preferred_element_type=F32)


def _ffn(x, g_ffn, w_gate, w_up, w_down, layer):
    m, d = x.shape
    f = w_gate.shape[2]
    tm = min(ROW_TILE, m)
    tf = FFN_TILE
    return pl.pallas_call(
        _ffn_kernel,
        grid=(m // tm, f // tf),
        in_specs=[pl.BlockSpec((tm, d), lambda i, j: (i, 0)),
                  pl.BlockSpec((1, d), lambda i, j: (0, 0)),
                  pl.BlockSpec((None, d, tf), lambda i, j: (layer, 0, j)),
                  pl.BlockSpec((None, d, tf), lambda i, j: (layer, 0, j)),
                  pl.BlockSpec((None, tf, d), lambda i, j: (layer, j, 0))],
        out_specs=pl.BlockSpec((tm, d), lambda i, j: (i, 0)),
        out_shape=jax.ShapeDtypeStruct((m, d), F32),
        scratch_shapes=[pltpu.VMEM((tm, d), BF16)],
        compiler_params=_params("arbitrary", "arbitrary"),
        name="ffn",
    )(x, g_ffn.reshape(1, d), w_gate, w_up, w_down)


def kernel(x_prompt, x_sample, cache_a_k, cache_a_v, cache_b_k, cache_b_v, cache_b_logf,
           g_attn, w_qkv, g_q, g_k, w_o, rel_table, w_f, b_f, g_ffn, w_gate, w_up, w_down):
    batch, seq, d = x_prompt.shape
    dec_batch, n_new, _ = x_sample.shape
    depth = g_attn.shape[0]
    n_a, n_b = cache_a_k.shape[0], cache_b_k.shape[0]
    keep = min(WINDOW, seq)
    h, hd = N_HEADS, HEAD_DIM
    la, past = cache_a_k.shape[2], cache_b_k.shape[2]
    assert d == h * hd and keep == WINDOW and la == WINDOW

    w_qkv_b = w_qkv.astype(BF16)
    w_o_b = w_o.astype(BF16)
    w_f_b = w_f.astype(BF16)
    w_gate_b = w_gate.astype(BF16)
    w_up_b = w_up.astype(BF16)
    w_down_b = w_down.astype(BF16)

    cache_a_k = cache_a_k.reshape(n_a, dec_batch, la * h, hd)
    cache_a_v = cache_a_v.reshape(n_a, dec_batch, la * h, hd)
    cache_b_k = cache_b_k.reshape(n_b, dec_batch, past * h, hd)
    cache_b_v = cache_b_v.reshape(n_b, dec_batch, past * h, hd)

    yp = x_prompt.reshape(batch * seq, d)
    ys = x_sample.reshape(dec_batch * n_new, d)
    a_kv, b_kv = None, None
    b_fp, a_ks, a_vs, b_ks, b_vs, b_fs = [], [], [], [], [], []

    for layer in range(depth):
        idx = layer // 2
        common = (g_attn[layer], w_qkv_b, layer, g_q[layer], g_k[layer])
        if layer % 2 == 0:
            qb, kb, vb, *a_kv = _qkv_proj(yp, *common, sample=False, kv_last=keep, seq=seq,
                                          kv_slots=n_a, kv_slot=idx, kv_prev=a_kv)
            qs, ks, vs = _qkv_proj(ys, *common, sample=True)
            bias_p, bias_s = _rel_bias_blocks(rel_table[idx], n_new)
            mp = _band_attention(qb, kb, vb, bias_p, batch, seq)
            ms = _band_attention_sample(qs, ks, vs, cache_a_k, cache_a_v, idx, bias_s, n_new)
            a_ks.append(ks)
            a_vs.append(vs)
        else:
            fw = (w_f_b[idx], b_f[idx])
            qb, kb, vb, k_all, v_all, lf = _qkv_proj(
                yp, *common, fw, sample=False, q_scale=SCALE * LOG2E,
                kv_slots=n_b, kv_slot=idx, kv_prev=b_kv)
            b_kv = [k_all, v_all]
            qs, ks, vs, lfs = _qkv_proj(ys, *common, fw, sample=True)
            mp = _fox_attention(qb, kb, vb, _cumsum_lanes(lf.T, seq), batch, seq)
            lf_cache = cache_b_logf[idx].transpose(0, 2, 1).reshape(dec_batch * h, past)
            c_cache = _cumsum_lanes(lf_cache, past).reshape(dec_batch, h, past)
            lf_new = lfs.reshape(dec_batch, n_new, h).transpose(0, 2, 1)
            ms = _fox_attention_sample(qs, ks, vs, cache_b_k, cache_b_v, idx, c_cache, lf_new, n_new)
            b_fp.append(lf.reshape(batch, seq, h))
            b_ks.append(ks)
            b_vs.append(vs)
            b_fs.append(lfs.reshape(dec_batch, n_new, h))
        yp = _out_proj(mp, w_o_b, layer, yp)
        ys = _out_proj(ms, w_o_b, layer, ys)
        yp = _ffn(yp, g_ffn[layer], w_gate_b, w_up_b, w_down_b, layer)
        ys = _ffn(ys, g_ffn[layer], w_gate_b, w_up_b, w_down_b, layer)

    def new_kv(parts):
        return jnp.stack(parts).reshape(len(parts), dec_batch, n_new, h, hd)

    return (yp.reshape(batch, seq, d), ys.reshape(dec_batch, n_new, d),
            a_kv[0].reshape(n_a, batch, keep, h, hd), a_kv[1].reshape(n_a, batch, keep, h, hd),
            b_kv[0].reshape(n_b, batch, seq, h, hd), b_kv[1].reshape(n_b, batch, seq, h, hd),
            jnp.stack(b_fp), new_kv(a_ks), new_kv(a_vs), new_kv(b_ks), new_kv(b_vs), jnp.stack(b_fs))
```

```python
import functools
import math

import jax
import jax.numpy as jnp
from jax import lax
from jax.experimental import pallas as pl
from jax.experimental.pallas import tpu as pltpu

N_HEADS = 16
HEAD_DIM = 128
CHUNK = 64
N_LEFT_CHUNKS = 8
WINDOW = N_LEFT_CHUNKS * CHUNK
MAX_REL = 128
EPS = 1e-6
SCALE = HEAD_DIM ** -0.5
LOG2E = math.log2(math.e)
NEG = -1e30

BF16 = jnp.bfloat16
F32 = jnp.float32

LANES = 128
HEADS_PER_TILE = 4
ROW_TILE = 512
FFN_TILE = 512
BAND_Q = 256
BAND_STEP = 2048
FOX_BLOCK = 512
FOX_CHUNK = 1024
FOX_STEP = 2048
FOX_MAX_SLACK = 96.0
SCAN_BLOCK = 512
ROLL_WIDTH = 1024
VMEM_LIMIT = 56 * 1024 * 1024

_NT = (((1,), (1,)), ((), ()))


def _params(*sem):
    return pltpu.CompilerParams(dimension_semantics=sem, vmem_limit_bytes=VMEM_LIMIT)


def _log_sigmoid(z):
    return jnp.minimum(z, 0.0) - jnp.log1p(jnp.exp(-jnp.abs(z)))


def _split3(x):
    hi = x.astype(BF16).astype(F32)
    r1 = x - hi
    mid = r1.astype(BF16).astype(F32)
    lo = r1 - mid
    return jnp.concatenate([hi, mid, lo], axis=0).astype(BF16)


def _head_rows(head, n_rows):
    return pl.ds(head, n_rows, stride=N_HEADS)


def _qkv_kernel(*refs, forget, sample, q_scale, n_alias):
    x_ref, ga_ref, wq_ref, wk_ref, wv_ref, gq_ref, gk_ref = refs[:7]
    rest = list(refs[7:])
    if forget:
        wf_ref, bf_ref = rest[:2]
        rest = rest[2:]
    rest = rest[n_alias:]
    if sample:
        q_out, k_out, v_out = rest[:3]
        rest = rest[3:]
    else:
        qb_out, kb_out, vb_out, k_out, v_out = rest[:5]
        rest = rest[5:]
    if forget:
        lf_out = rest[0]
        rest = rest[1:]
    (xn_ref,) = rest
    j = pl.program_id(1)
    tm = x_ref.shape[0]

    @pl.when(j == 0)
    def _():
        x = x_ref[...]
        ms = jnp.mean(x * x, axis=-1, keepdims=True)
        xn = (x * lax.rsqrt(ms + EPS) * ga_ref[...]).astype(BF16)
        xn_ref[...] = xn
        if forget:
            z = jnp.dot(xn, wf_ref[...], preferred_element_type=F32) + bf_ref[...]
            lf_out[...] = _log_sigmoid(z)

    xn = xn_ref[...]
    q = jnp.dot(xn, wq_ref[...], preferred_element_type=F32)
    k = jnp.dot(xn, wk_ref[...], preferred_element_type=F32)
    v = jnp.dot(xn, wv_ref[...], preferred_element_type=F32)
    gq = gq_ref[...]
    gk = gk_ref[...]
    for hh in range(HEADS_PER_TILE):
        sl = slice(hh * HEAD_DIM, (hh + 1) * HEAD_DIM)
        qh = q[:, sl]
        kh = k[:, sl]
        vh = v[:, sl]
        qn = qh * lax.rsqrt(jnp.mean(qh * qh, axis=-1, keepdims=True) + EPS) * gq * q_scale
        kn = kh * lax.rsqrt(jnp.mean(kh * kh, axis=-1, keepdims=True) + EPS) * gk
        if sample:
            q_out[:, sl] = qn
            k_out[:, sl] = kn
            v_out[:, sl] = vh
        else:
            qb_out[hh] = qn.astype(BF16)
            kb_out[hh] = kn.astype(BF16)
            vb_out[hh] = vh.astype(BF16)
            rows = _head_rows(j * HEADS_PER_TILE + hh, tm)
            k_out[rows, :] = kn
            v_out[rows, :] = vh


def _qkv_proj(x, g_attn, w_qkv, layer, g_q, g_k, forget_w=None, *, sample, q_scale=SCALE,
              kv_last=None, seq=None, kv_slots=None, kv_slot=0, kv_prev=None):
    m, d = x.shape
    tm = min(ROW_TILE, m)
    tn = HEADS_PER_TILE * HEAD_DIM
    nj = d // tn
    grid = (m // tm, nj)
    forget = forget_w is not None

    in_specs = [
        pl.BlockSpec((tm, d), lambda i, j: (i, 0)),
        pl.BlockSpec((1, d), lambda i, j: (0, 0)),
        pl.BlockSpec((None, d, tn), lambda i, j: (layer, 0, j)),
        pl.BlockSpec((None, d, tn), lambda i, j: (layer, 0, nj + j)),
        pl.BlockSpec((None, d, tn), lambda i, j: (layer, 0, 2 * nj + j)),
        pl.BlockSpec((1, HEAD_DIM), lambda i, j: (0, 0)),
        pl.BlockSpec((1, HEAD_DIM), lambda i, j: (0, 0)),
    ]
    args = [x, g_attn.reshape(1, d), w_qkv, w_qkv, w_qkv,
            g_q.reshape(1, HEAD_DIM), g_k.reshape(1, HEAD_DIM)]
    if forget:
        w_f, b_f = forget_w
        in_specs += [pl.BlockSpec((d, N_HEADS), lambda i, j: (0, 0)),
                     pl.BlockSpec((1, N_HEADS), lambda i, j: (0, 0))]
        args += [w_f, b_f.reshape(1, N_HEADS)]

    aliases = {}
    if sample:
        tok_spec = pl.BlockSpec((tm, tn), lambda i, j: (i, j))
        out_shape = [jax.ShapeDtypeStruct((m, d), F32)] * 3
        out_specs = [tok_spec] * 3
    else:
        hm_shape = jax.ShapeDtypeStruct((N_HEADS, m, HEAD_DIM), BF16)
        hm_spec = pl.BlockSpec((HEADS_PER_TILE, tm, HEAD_DIM), lambda i, j: (j, i, 0))
        if kv_last is None:
            kv_rows = m
            kv_spec = pl.BlockSpec((None, tm * N_HEADS, HEAD_DIM), lambda i, j: (kv_slot, i, 0))
        else:
            assert kv_last == tm and seq % tm == 0
            per_seq = seq // tm
            kv_rows = m // seq * kv_last
            kv_spec = pl.BlockSpec((None, tm * N_HEADS, HEAD_DIM),
                                   lambda i, j: (kv_slot, i // per_seq, 0))
        kv_shape = jax.ShapeDtypeStruct((kv_slots, kv_rows * N_HEADS, HEAD_DIM), F32)
        out_shape = [hm_shape] * 3 + [kv_shape] * 2
        out_specs = [hm_spec] * 3 + [kv_spec] * 2
        if kv_prev is not None:
            aliases = {len(args): 3, len(args) + 1: 4}
            in_specs += [pl.BlockSpec(memory_space=pl.ANY)] * 2
            args += list(kv_prev)
    if forget:
        out_shape += [jax.ShapeDtypeStruct((m, N_HEADS), F32)]
        out_specs += [pl.BlockSpec((tm, N_HEADS), lambda i, j: (i, 0))]

    return pl.pallas_call(
        functools.partial(_qkv_kernel, forget=forget, sample=sample, q_scale=q_scale,
                          n_alias=len(aliases)),
        grid=grid,
        in_specs=in_specs,
        out_specs=out_specs,
        out_shape=out_shape,
        scratch_shapes=[pltpu.VMEM((tm, d), BF16)],
        input_output_aliases=aliases,
        compiler_params=_params("arbitrary", "arbitrary"),
        name="qkv_sample" if sample else "qkv_prompt",
    )(*args)


def _cumsum_kernel(x_ref, u_ref, o_ref, carry_ref, *, blocks_per_seq):
    @pl.when(pl.program_id(0) % blocks_per_seq == 0)
    def _():
        carry_ref[...] = jnp.zeros_like(carry_ref)

    x = x_ref[...]
    r, t = x.shape
    s = jnp.dot(_split3(x), u_ref[...], preferred_element_type=F32)
    c = s[:r] + s[r:2 * r] + s[2 * r:] + carry_ref[:, :1]
    o_ref[...] = c
    carry_ref[...] = jnp.broadcast_to(c[:, t - 1:t], carry_ref.shape)


def _cumsum_lanes(x, seq):
    r, l = x.shape
    t = min(SCAN_BLOCK, seq)
    upper = jnp.triu(jnp.ones((t, t), BF16))
    return pl.pallas_call(
        functools.partial(_cumsum_kernel, blocks_per_seq=seq // t),
        grid=(l // t,),
        in_specs=[pl.BlockSpec((r, t), lambda i: (0, i)),
                  pl.BlockSpec((t, t), lambda i: (0, 0))],
        out_specs=pl.BlockSpec((r, t), lambda i: (0, i)),
        out_shape=jax.ShapeDtypeStruct((r, l), F32),
        scratch_shapes=[pltpu.VMEM((r, LANES), F32)],
        compiler_params=_params("arbitrary"),
        name="cumsum_lanes",
    )(x, upper)


def _bias_kernel(xp_ref, xs_ref, bp_ref, bs_ref):
    def toeplitz(row, n_rows, lo, hi):
        base = jnp.broadcast_to(row, (8, ROLL_WIDTH))
        blocks = []
        for g in range(n_rows // 8):
            rolled = pltpu.roll(base, 8 * g, 1, stride=1, stride_axis=0)
            blocks.append(rolled[:, lo:hi])
        return jnp.concatenate(blocks, axis=0)

    row_c = lax.broadcasted_iota(jnp.int32, (BAND_Q, 3 * BAND_Q), 0) // CHUNK
    col = lax.broadcasted_iota(jnp.int32, (BAND_Q, 3 * BAND_Q), 1)
    col_c = (col % BAND_Q) // CHUNK
    visible = ((col >= BAND_Q) | (row_c <= col_c)) & ((col < 2 * BAND_Q) | (col_c <= row_c))
    bp_ref[0] = jnp.where(visible, toeplitz(xp_ref[0], BAND_Q, BAND_Q, ROLL_WIDTH) * LOG2E, NEG)
    bs_ref[0] = toeplitz(xs_ref[0], bs_ref.shape[1], LANES, LANES + bs_ref.shape[2])


def _rel_bias_blocks(table, n_new):
    assert 2 * BAND_Q == WINDOW and 3 * BAND_Q + BAND_Q == ROLL_WIDTH
    rev = table[:, ::-1]
    n_rel = table.shape[1]
    lead_p = 3 * BAND_Q - MAX_REL
    lead_s = WINDOW + LANES - MAX_REL
    row_p = jnp.pad(rev, ((0, 0), (lead_p, ROLL_WIDTH - n_rel - lead_p)), mode="edge")
    row_s = jnp.pad(rev, ((0, 0), (lead_s, ROLL_WIDTH - n_rel - lead_s)), mode="edge")
    h = table.shape[0]
    ws = WINDOW + LANES
    return pl.pallas_call(
        _bias_kernel,
        grid=(h,),
        in_specs=[pl.BlockSpec((1, 1, ROLL_WIDTH), lambda i: (i, 0, 0)),
                  pl.BlockSpec((1, 1, ROLL_WIDTH), lambda i: (i, 0, 0))],
        out_specs=[pl.BlockSpec((1, BAND_Q, 3 * BAND_Q), lambda i: (i, 0, 0)),
                   pl.BlockSpec((1, n_new, ws), lambda i: (i, 0, 0))],
        out_shape=[jax.ShapeDtypeStruct((h, BAND_Q, 3 * BAND_Q), F32),
                   jax.ShapeDtypeStruct((h, n_new, ws), F32)],
        compiler_params=_params("arbitrary"),
        name="rel_bias",
    )(row_p.reshape(h, 1, ROLL_WIDTH), row_s.reshape(h, 1, ROLL_WIDTH))


def _band_kernel(q_ref, k_ref, v_ref, b_ref, o_ref):
    tq = BAND_Q
    n_sub = q_ref.shape[1] // tq
    step = pl.program_id(1)
    for sub in range(n_sub):
        i = step * n_sub + sub
        q = q_ref[0, sub * tq:(sub + 1) * tq, :]
        s_parts, v_parts = [], []
        for t in range(3):
            start = pl.multiple_of(jnp.maximum(i - 2 + t, 0) * tq, tq)
            kb = k_ref[0, pl.ds(start, tq), :]
            v_parts.append(v_ref[0, pl.ds(start, tq), :])
            s = lax.dot_general(q, kb, _NT, preferred_element_type=F32)
            s = s + b_ref[0, :, t * tq:(t + 1) * tq]
            if sub + t < 2:
                s = s + jnp.where(i + t >= 2, 0.0, NEG)
            s_parts.append(s)
        s = jnp.concatenate(s_parts, axis=1)
        m = jnp.max(s, axis=-1, keepdims=True)
        p = jnp.exp2(s - m)
        l = jnp.sum(p, axis=-1, keepdims=True)
        pb = p.astype(BF16)
        acc = jnp.dot(pb[:, :tq], v_parts[0], preferred_element_type=F32)
        acc += jnp.dot(pb[:, tq:2 * tq], v_parts[1], preferred_element_type=F32)
        acc += jnp.dot(pb[:, 2 * tq:], v_parts[2], preferred_element_type=F32)
        o_ref[0, sub * tq:(sub + 1) * tq, :] = (acc / l).astype(o_ref.dtype)


def _band_attention(qb, kb, vb, bias, batch, seq):
    h, m, hd = qb.shape
    step = min(BAND_STEP, seq)
    nq = seq // step
    return pl.pallas_call(
        _band_kernel,
        grid=(h * batch, nq),
        in_specs=[pl.BlockSpec((1, step, hd), lambda g, i: (g // batch, (g % batch) * nq + i, 0)),
                  pl.BlockSpec((1, seq, hd), lambda g, i: (g // batch, g % batch, 0)),
                  pl.BlockSpec((1, seq, hd), lambda g, i: (g // batch, g % batch, 0)),
                  pl.BlockSpec((1, BAND_Q, 3 * BAND_Q), lambda g, i: (g // batch, 0, 0))],
        out_specs=pl.BlockSpec((1, step, hd), lambda g, i: (g // batch, (g % batch) * nq + i, 0)),
        out_shape=jax.ShapeDtypeStruct((h, m, hd), BF16),
        compiler_params=_params("arbitrary", "arbitrary"),
        name="band_attention",
    )(qb, kb, vb, bias)


def _band_sample_kernel(q_ref, kn_ref, vn_ref, kc_ref, vc_ref, b_ref, o_ref):
    n = q_ref.shape[0]
    lc = kc_ref.shape[0] // N_HEADS
    for h in range(N_HEADS):
        sl = slice(h * HEAD_DIM, (h + 1) * HEAD_DIM)
        rows = _head_rows(h, lc)
        q = q_ref[:, sl].astype(BF16)
        s_c = lax.dot_general(q, kc_ref[rows, :].astype(BF16), _NT, preferred_element_type=F32)
        s_n = lax.dot_general(q, kn_ref[:, sl].astype(BF16), _NT, preferred_element_type=F32)
        s_c = s_c + b_ref[h, :, :lc]
        s_n = s_n + b_ref[h, :, lc:lc + n]
        m = jnp.maximum(jnp.max(s_c, axis=-1, keepdims=True), jnp.max(s_n, axis=-1, keepdims=True))
        p_c = jnp.exp(s_c - m)
        p_n = jnp.exp(s_n - m)
        l = jnp.sum(p_c, axis=-1, keepdims=True) + jnp.sum(p_n, axis=-1, keepdims=True)
        acc = jnp.dot(p_c.astype(BF16), vc_ref[rows, :].astype(BF16), preferred_element_type=F32)
        acc += jnp.dot(p_n.astype(BF16), vn_ref[:, sl].astype(BF16), preferred_element_type=F32)
        o_ref[h] = (acc / l).astype(o_ref.dtype)


def _band_attention_sample(q, k_new, v_new, k_cache, v_cache, idx, bias, n):
    m, d = q.shape
    _, b, rows, hd = k_cache.shape
    tok = pl.BlockSpec((n, d), lambda i: (i, 0))
    cache = pl.BlockSpec((None, None, rows, hd), lambda i: (idx, i, 0, 0))
    return pl.pallas_call(
        _band_sample_kernel,
        grid=(b,),
        in_specs=[tok, tok, tok, cache, cache,
                  pl.BlockSpec(bias.shape, lambda i: (0, 0, 0))],
        out_specs=pl.BlockSpec((N_HEADS, n, HEAD_DIM), lambda i: (0, i, 0)),
        out_shape=jax.ShapeDtypeStruct((N_HEADS, m, HEAD_DIM), BF16),
        compiler_params=_params("arbitrary"),
        name="band_attention_sample",
    )(q, k_new, v_new, k_cache, v_cache, bias)


def _online_softmax_step(s, carry, v, upper=None):
    m, l, acc = carry
    m_new = jnp.maximum(m, jnp.max(s, axis=-1, keepdims=True) if upper is None else upper)
    alpha = jnp.exp2(m - m_new)
    p = jnp.exp2(s - m_new)
    l = alpha * l + jnp.sum(p, axis=-1, keepdims=True)
    acc = alpha * acc + jnp.dot(p.astype(BF16), v, preferred_element_type=F32)
    return m_new, l, acc


def _fox_kernel(q_ref, k_ref, v_ref, c_ref, o_ref, kmax_ref):
    i = pl.program_id(1)
    tq, hd = q_ref.shape[1], q_ref.shape[2]
    seq = k_ref.shape[1]
    blk = min(FOX_BLOCK, tq)
    tk = min(FOX_CHUNK, tq)
    n, per = tq // blk, tk // blk
    qs = [q_ref[0, a * blk:(a + 1) * blk, :] for a in range(n)]

    @pl.when(i == 0)
    def _():
        def longest(t, best):
            kk = k_ref[0, pl.ds(pl.multiple_of(t * blk, blk), blk), :].astype(F32)
            return jnp.maximum(best, jnp.sum(kk * kk, axis=-1, keepdims=True))
        best = lax.fori_loop(0, seq // blk, longest, jnp.zeros((blk, 1), F32))
        kmax_ref[...] = jnp.broadcast_to(jnp.sqrt(jnp.max(best, axis=0, keepdims=True)),
                                         kmax_ref.shape)

    kmax = kmax_ref[0:1, 0:1]
    reach = []
    for q in qs:
        qf = q.astype(F32)
        reach.append(jnp.sqrt(jnp.sum(qf * qf, axis=-1, keepdims=True)) * kmax)
    slack = 2.0 * functools.reduce(jnp.maximum, [jnp.max(r) for r in reach])

    base = pl.multiple_of(i * tq, tq)
    causal = (lax.broadcasted_iota(jnp.int32, (blk, blk), 1)
              <= lax.broadcasted_iota(jnp.int32, (blk, blk), 0))

    def logits(q, start, size, first_block):
        k = k_ref[0, pl.ds(start, size), :]
        s = lax.dot_general(q, k, _NT, preferred_element_type=F32)
        cs = [c_ref[0, pl.ds(first_block + t, 1), :] * LOG2E for t in range(size // blk)]
        parts = [s[:, t * blk:(t + 1) * blk] - cs[t] for t in range(size // blk)]
        c_min = functools.reduce(jnp.minimum, [jnp.min(c, axis=-1, keepdims=True) for c in cs])
        return jnp.concatenate(parts, axis=1), v_ref[0, pl.ds(start, size), :], c_min, cs[-1]

    def attend(bounded):
        def upper(a, c_min):
            return reach[a] - c_min if bounded else None

        def full_chunk(j, carry):
            start = pl.multiple_of(j * tk, tk)
            out = []
            for a in range(n):
                s, v, c_min, _ = logits(qs[a], start, tk, j * per)
                out.append(_online_softmax_step(s, carry[a], v, upper(a, c_min)))
            return tuple(out)

        init = (jnp.full((blk, 1), NEG, F32), jnp.zeros((blk, 1), F32), jnp.zeros((blk, hd), F32))
        carry = lax.fori_loop(0, i * (tq // tk), full_chunk, (init,) * n)

        for a in range(n):
            state = carry[a]
            if a > 0:
                s, v, c_min, _ = logits(qs[a], base, a * blk, i * n)
                state = _online_softmax_step(s, state, v, upper(a, c_min))
            s, v, _, c_own = logits(qs[a], base + a * blk, blk, i * n + a)
            c_vis = jnp.min(jnp.where(causal, c_own, -NEG), axis=-1, keepdims=True)
            _, l, acc = _online_softmax_step(jnp.where(causal, s, NEG), state, v, upper(a, c_vis))
            o_ref[0, a * blk:(a + 1) * blk, :] = (acc / l).astype(o_ref.dtype)

    use_bound = slack <= FOX_MAX_SLACK
    pl.when(use_bound)(lambda: attend(True))
    pl.when(jnp.logical_not(use_bound))(lambda: attend(False))


def _fox_attention(qb, kb, vb, c_rows, batch, seq):
    h, m, hd = qb.shape
    t = min(FOX_STEP, seq)
    nq = seq // t
    blk = min(FOX_BLOCK, t)
    c_blocks = c_rows.reshape(h, m // blk, blk)
    return pl.pallas_call(
        _fox_kernel,
        grid=(h * batch, nq),
        in_specs=[pl.BlockSpec((1, t, hd), lambda g, i: (g // batch, (g % batch) * nq + i, 0)),
                  pl.BlockSpec((1, seq, hd), lambda g, i: (g // batch, g % batch, 0)),
                  pl.BlockSpec((1, seq, hd), lambda g, i: (g // batch, g % batch, 0)),
                  pl.BlockSpec((1, seq // blk, blk), lambda g, i: (g // batch, g % batch, 0))],
        out_specs=pl.BlockSpec((1, t, hd), lambda g, i: (g // batch, (g % batch) * nq + i, 0)),
        out_shape=jax.ShapeDtypeStruct((h, m, hd), BF16),
        scratch_shapes=[pltpu.VMEM((8, LANES), F32)],
        compiler_params=_params("arbitrary", "arbitrary"),
        name="fox_attention",
    )(qb, kb, vb, c_blocks)


def _fox_sample_kernel(q_ref, kn_ref, vn_ref, kc_ref, vc_ref, cc_ref, lfn_ref, u_ref, o_ref):
    n = q_ref.shape[0]
    c_cache = cc_ref[0]
    lc = c_cache.shape[1]
    s3 = jnp.dot(_split3(lfn_ref[0]), u_ref[...], preferred_element_type=F32)
    c_new = s3[:N_HEADS] + s3[N_HEADS:2 * N_HEADS] + s3[2 * N_HEADS:] + c_cache[:, lc - 1:lc]
    row = lax.broadcasted_iota(jnp.int32, (n, n), 0)
    col = lax.broadcasted_iota(jnp.int32, (n, n), 1)
    for h in range(N_HEADS):
        sl = slice(h * HEAD_DIM, (h + 1) * HEAD_DIM)
        rows = _head_rows(h, lc)
        q = q_ref[:, sl].astype(BF16)
        s_c = lax.dot_general(q, kc_ref[rows, :].astype(BF16), _NT, preferred_element_type=F32)
        s_n = lax.dot_general(q, kn_ref[:, sl].astype(BF16), _NT, preferred_element_type=F32)
        s_c = s_c - c_cache[h:h + 1, :]
        s_n = jnp.where(col <= row, s_n - c_new[h:h + 1, :], NEG)
        m = jnp.maximum(jnp.max(s_c, axis=-1, keepdims=True), jnp.max(s_n, axis=-1, keepdims=True))
        p_c = jnp.exp(s_c - m)
        p_n = jnp.exp(s_n - m)
        l = jnp.sum(p_c, axis=-1, keepdims=True) + jnp.sum(p_n, axis=-1, keepdims=True)
        acc = jnp.dot(p_c.astype(BF16), vc_ref[rows, :].astype(BF16), preferred_element_type=F32)
        acc += jnp.dot(p_n.astype(BF16), vn_ref[:, sl].astype(BF16), preferred_element_type=F32)
        o_ref[h] = (acc / l).astype(o_ref.dtype)


def _fox_attention_sample(q, k_new, v_new, k_cache, v_cache, idx, c_cache, lf_new, n):
    m, d = q.shape
    _, b, rows, hd = k_cache.shape
    lc = rows // N_HEADS
    tok = pl.BlockSpec((n, d), lambda i: (i, 0))
    cache = pl.BlockSpec((None, None, rows, hd), lambda i: (idx, i, 0, 0))
    upper = jnp.triu(jnp.ones((n, n), BF16))
    return pl.pallas_call(
        _fox_sample_kernel,
        grid=(b,),
        in_specs=[tok, tok, tok, cache, cache,
                  pl.BlockSpec((1, N_HEADS, lc), lambda i: (i, 0, 0)),
                  pl.BlockSpec((1, N_HEADS, n), lambda i: (i, 0, 0)),
                  pl.BlockSpec((n, n), lambda i: (0, 0))],
        out_specs=pl.BlockSpec((N_HEADS, n, HEAD_DIM), lambda i: (0, i, 0)),
        out_shape=jax.ShapeDtypeStruct((N_HEADS, m, HEAD_DIM), BF16),
        compiler_params=_params("arbitrary"),
        name="fox_attention_sample",
    )(q, k_new, v_new, k_cache, v_cache, c_cache, lf_new, upper)


def _out_proj_kernel(a_ref, w_ref, x_ref, o_ref):
    a = jnp.concatenate([a_ref[h] for h in range(N_HEADS)], axis=1)
    o_ref[...] = x_ref[...] + jnp.dot(a, w_ref[...], preferred_element_type=F32)


def _out_proj(attn, w_o, layer, x):
    m, d = x.shape
    tm = min(ROW_TILE, m)
    return pl.pallas_call(
        _out_proj_kernel,
        grid=(m // tm,),
        in_specs=[pl.BlockSpec((N_HEADS, tm, HEAD_DIM), lambda i: (0, i, 0)),
                  pl.BlockSpec((None, d, d), lambda i: (layer, 0, 0)),
                  pl.BlockSpec((tm, d), lambda i: (i, 0))],
        out_specs=pl.BlockSpec((tm, d), lambda i: (i, 0)),
        out_shape=jax.ShapeDtypeStruct((m, d), F32),
        compiler_params=_params("arbitrary"),
        name="out_proj",
    )(attn, w_o, x)


def _ffn_kernel(x_ref, g_ref, wg_ref, wu_ref, wd_ref, o_ref, xn_ref):
    @pl.when(pl.program_id(1) == 0)
    def _():
        x = x_ref[...]
        ms = jnp.mean(x * x, axis=-1, keepdims=True)
        xn_ref[...] = (x * lax.rsqrt(ms + EPS) * g_ref[...]).astype(BF16)
        o_ref[...] = x

    xn = xn_ref[...]
    gate = jnp.dot(xn, wg_ref[...], preferred_element_type=F32)
    up = jnp.dot(xn, wu_ref[...], preferred_element_type=F32)
    hidden = (gate * jax.nn.sigmoid(gate) * up).astype(BF16)
    o_ref[...] += jnp.dot(hidden, wd_ref[...], preferred_element_type=F32)


def _ffn(x, g_ffn, w_gate, w_up, w_down, layer):
    m, d = x.shape
    f = w_gate.shape[2]
    tm = min(ROW_TILE, m)
    tf = FFN_TILE
    return pl.pallas_call(
        _ffn_kernel,
        grid=(m // tm, f // tf),
        in_specs=[pl.BlockSpec((tm, d), lambda i, j: (i, 0)),
                  pl.BlockSpec((1, d), lambda i, j: (0, 0)),
                  pl.BlockSpec((None, d, tf), lambda i, j: (layer, 0, j)),
                  pl.BlockSpec((None, d, tf), lambda i, j: (layer, 0, j)),
                  pl.BlockSpec((None, tf, d), lambda i, j: (layer, j, 0))],
        out_specs=pl.BlockSpec((tm, d), lambda i, j: (i, 0)),
        out_shape=jax.ShapeDtypeStruct((m, d), F32),
        scratch_shapes=[pltpu.VMEM((tm, d), BF16)],
        compiler_params=_params("arbitrary", "arbitrary"),
        name="ffn",
    )(x, g_ffn.reshape(1, d), w_gate, w_up, w_down)


def kernel(x_prompt, x_sample, cache_a_k, cache_a_v, cache_b_k, cache_b_v, cache_b_logf,
           g_attn, w_qkv, g_q, g_k, w_o, rel_table, w_f, b_f, g_ffn, w_gate, w_up, w_down):
    batch, seq, d = x_prompt.shape
    dec_batch, n_new, _ = x_sample.shape
    depth = g_attn.shape[0]
    n_a, n_b = cache_a_k.shape[0], cache_b_k.shape[0]
    keep = min(WINDOW, seq)
    h, hd = N_HEADS, HEAD_DIM
    la, past = cache_a_k.shape[2], cache_b_k.shape[2]
    assert d == h * hd and keep == WINDOW and la == WINDOW

    w_qkv_b = w_qkv.astype(BF16)
    w_o_b = w_o.astype(BF16)
    w_f_b = w_f.astype(BF16)
    w_gate_b = w_gate.astype(BF16)
    w_up_b = w_up.astype(BF16)
    w_down_b = w_down.astype(BF16)

    cache_a_k = cache_a_k.reshape(n_a, dec_batch, la * h, hd)
    cache_a_v = cache_a_v.reshape(n_a, dec_batch, la * h, hd)
    cache_b_k = cache_b_k.reshape(n_b, dec_batch, past * h, hd)
    cache_b_v = cache_b_v.reshape(n_b, dec_batch, past * h, hd)

    yp = x_prompt.reshape(batch * seq, d)
    ys = x_sample.reshape(dec_batch * n_new, d)
    a_kv, b_kv = None, None
    b_fp, a_ks, a_vs, b_ks, b_vs, b_fs = [], [], [], [], [], []

    for layer in range(depth):
        idx = layer // 2
        common = (g_attn[layer], w_qkv_b, layer, g_q[layer], g_k[layer])
        if layer % 2 == 0:
            qb, kb, vb, *a_kv = _qkv_proj(yp, *common, sample=False, q_scale=SCALE * LOG2E,
                                          kv_last=keep, seq=seq,
                                          kv_slots=n_a, kv_slot=idx, kv_prev=a_kv)
            qs, ks, vs = _qkv_proj(ys, *common, sample=True)
            bias_p, bias_s = _rel_bias_blocks(rel_table[idx], n_new)
            mp = _band_attention(qb, kb, vb, bias_p, batch, seq)
            ms = _band_attention_sample(qs, ks, vs, cache_a_k, cache_a_v, idx, bias_s, n_new)
            a_ks.append(ks)
            a_vs.append(vs)
        else:
            fw = (w_f_b[idx], b_f[idx])
            qb, kb, vb, k_all, v_all, lf = _qkv_proj(
                yp, *common, fw, sample=False, q_scale=SCALE * LOG2E,
                kv_slots=n_b, kv_slot=idx, kv_prev=b_kv)
            b_kv = [k_all, v_all]
            qs, ks, vs, lfs = _qkv_proj(ys, *common, fw, sample=True)
            mp = _fox_attention(qb, kb, vb, _cumsum_lanes(lf.T, seq), batch, seq)
            lf_cache = cache_b_logf[idx].transpose(0, 2, 1).reshape(dec_batch * h, past)
            c_cache = _cumsum_lanes(lf_cache, past).reshape(dec_batch, h, past)
            lf_new = lfs.reshape(dec_batch, n_new, h).transpose(0, 2, 1)
            ms = _fox_attention_sample(qs, ks, vs, cache_b_k, cache_b_v, idx, c_cache, lf_new, n_new)
            b_fp.append(lf.reshape(batch, seq, h))
            b_ks.append(ks)
            b_vs.append(vs)
            b_fs.append(lfs.reshape(dec_batch, n_new, h))
        yp = _out_proj(mp, w_o_b, layer, yp)
        ys = _out_proj(ms, w_o_b, layer, ys)
        yp = _ffn(yp, g_ffn[layer], w_gate_b, w_up_b, w_down_b, layer)
        ys = _ffn(ys, g_ffn[layer], w_gate_b, w_up_b, w_down_b, layer)

    def new_kv(parts):
        return jnp.stack(parts).reshape(len(parts), dec_batch, n_new, h, hd)

    return (yp.reshape(batch, seq, d), ys.reshape(dec_batch, n_new, d),
            a_kv[0].reshape(n_a, batch, keep, h, hd), a_kv[1].reshape(n_a, batch, keep, h, hd),
            b_kv[0].reshape(n_b, batch, seq, h, hd), b_kv[1].reshape(n_b, batch, seq, h, hd),
            jnp.stack(b_fp), new_kv(a_ks), new_kv(a_vs), new_kv(b_ks), new_kv(b_vs), jnp.stack(b_fs))
```

```python
import functools
import math

import jax
import jax.numpy as jnp
from jax import lax
from jax.experimental import pallas as pl
from jax.experimental.pallas import tpu as pltpu

N_HEADS = 16
HEAD_DIM = 128
CHUNK = 64
N_LEFT_CHUNKS = 8
WINDOW = N_LEFT_CHUNKS * CHUNK
MAX_REL = 128
EPS = 1e-6
SCALE = HEAD_DIM ** -0.5
LOG2E = math.log2(math.e)
NEG = -1e30

BF16 = jnp.bfloat16
F32 = jnp.float32

LANES = 128
HEADS_PER_TILE = 4
ROW_TILE = 512
FFN_TILE = 512
FFN_ROW_TILE = 1024
FFN_VMEM_LIMIT = 62 * 1024 * 1024
BAND_Q = 256
BAND_STEP = 2048
FOX_BLOCK = 512
FOX_CHUNK = 1024
FOX_STEP = 2048
FOX_MAX_SLACK = 96.0
SCAN_BLOCK = 512
ROLL_WIDTH = 1024
VMEM_LIMIT = 56 * 1024 * 1024

_NT = (((1,), (1,)), ((), ()))


def _params(*sem, vmem=VMEM_LIMIT):
    return pltpu.CompilerParams(dimension_semantics=sem, vmem_limit_bytes=vmem)


def _log_sigmoid(z):
    return jnp.minimum(z, 0.0) - jnp.log1p(jnp.exp(-jnp.abs(z)))


def _split3(x):
    hi = x.astype(BF16).astype(F32)
    r1 = x - hi
    mid = r1.astype(BF16).astype(F32)
    lo = r1 - mid
    return jnp.concatenate([hi, mid, lo], axis=0).astype(BF16)


def _head_rows(head, n_rows):
    return pl.ds(head, n_rows, stride=N_HEADS)


def _qkv_kernel(*refs, forget, sample, q_scale, n_alias):
    x_ref, ga_ref, wq_ref, wk_ref, wv_ref, gq_ref, gk_ref = refs[:7]
    rest = list(refs[7:])
    if forget:
        wf_ref, bf_ref = rest[:2]
        rest = rest[2:]
    rest = rest[n_alias:]
    if sample:
        q_out, k_out, v_out = rest[:3]
        rest = rest[3:]
    else:
        qb_out, kb_out, vb_out, k_out, v_out = rest[:5]
        rest = rest[5:]
    if forget:
        lf_out = rest[0]
        rest = rest[1:]
    (xn_ref,) = rest
    j = pl.program_id(1)
    tm = x_ref.shape[0]

    @pl.when(j == 0)
    def _():
        x = x_ref[...]
        ms = jnp.mean(x * x, axis=-1, keepdims=True)
        xn = (x * lax.rsqrt(ms + EPS) * ga_ref[...]).astype(BF16)
        xn_ref[...] = xn
        if forget:
            z = jnp.dot(xn, wf_ref[...], preferred_element_type=F32) + bf_ref[...]
            lf_out[...] = _log_sigmoid(z)

    xn = xn_ref[...]
    q = jnp.dot(xn, wq_ref[...], preferred_element_type=F32)
    k = jnp.dot(xn, wk_ref[...], preferred_element_type=F32)
    v = jnp.dot(xn, wv_ref[...], preferred_element_type=F32)
    gq = gq_ref[...]
    gk = gk_ref[...]
    for hh in range(HEADS_PER_TILE):
        sl = slice(hh * HEAD_DIM, (hh + 1) * HEAD_DIM)
        qh = q[:, sl]
        kh = k[:, sl]
        vh = v[:, sl]
        qn = qh * lax.rsqrt(jnp.mean(qh * qh, axis=-1, keepdims=True) + EPS) * gq * q_scale
        kn = kh * lax.rsqrt(jnp.mean(kh * kh, axis=-1, keepdims=True) + EPS) * gk
        if sample:
            q_out[:, sl] = qn
            k_out[:, sl] = kn
            v_out[:, sl] = vh
        else:
            qb_out[hh] = qn.astype(BF16)
            kb_out[hh] = kn.astype(BF16)
            vb_out[hh] = vh.astype(BF16)
            rows = _head_rows(j * HEADS_PER_TILE + hh, tm)
            k_out[rows, :] = kn
            v_out[rows, :] = vh


def _qkv_proj(x, g_attn, w_qkv, layer, g_q, g_k, forget_w=None, *, sample, q_scale=SCALE,
              kv_last=None, seq=None, kv_slots=None, kv_slot=0, kv_prev=None):
    m, d = x.shape
    tm = min(ROW_TILE, m)
    tn = HEADS_PER_TILE * HEAD_DIM
    nj = d // tn
    grid = (m // tm, nj)
    forget = forget_w is not None

    in_specs = [
        pl.BlockSpec((tm, d), lambda i, j: (i, 0)),
        pl.BlockSpec((1, d), lambda i, j: (0, 0)),
        pl.BlockSpec((None, d, tn), lambda i, j: (layer, 0, j)),
        pl.BlockSpec((None, d, tn), lambda i, j: (layer, 0, nj + j)),
        pl.BlockSpec((None, d, tn), lambda i, j: (layer, 0, 2 * nj + j)),
        pl.BlockSpec((1, HEAD_DIM), lambda i, j: (0, 0)),
        pl.BlockSpec((1, HEAD_DIM), lambda i, j: (0, 0)),
    ]
    args = [x, g_attn.reshape(1, d), w_qkv, w_qkv, w_qkv,
            g_q.reshape(1, HEAD_DIM), g_k.reshape(1, HEAD_DIM)]
    if forget:
        w_f, b_f = forget_w
        in_specs += [pl.BlockSpec((d, N_HEADS), lambda i, j: (0, 0)),
                     pl.BlockSpec((1, N_HEADS), lambda i, j: (0, 0))]
        args += [w_f, b_f.reshape(1, N_HEADS)]

    aliases = {}
    if sample:
        tok_spec = pl.BlockSpec((tm, tn), lambda i, j: (i, j))
        out_shape = [jax.ShapeDtypeStruct((m, d), F32)] * 3
        out_specs = [tok_spec] * 3
    else:
        hm_shape = jax.ShapeDtypeStruct((N_HEADS, m, HEAD_DIM), BF16)
        hm_spec = pl.BlockSpec((HEADS_PER_TILE, tm, HEAD_DIM), lambda i, j: (j, i, 0))
        if kv_last is None:
            kv_rows = m
            kv_spec = pl.BlockSpec((None, tm * N_HEADS, HEAD_DIM), lambda i, j: (kv_slot, i, 0))
        else:
            assert kv_last == tm and seq % tm == 0
            per_seq = seq // tm
            kv_rows = m // seq * kv_last
            kv_spec = pl.BlockSpec((None, tm * N_HEADS, HEAD_DIM),
                                   lambda i, j: (kv_slot, i // per_seq, 0))
        kv_shape = jax.ShapeDtypeStruct((kv_slots, kv_rows * N_HEADS, HEAD_DIM), F32)
        out_shape = [hm_shape] * 3 + [kv_shape] * 2
        out_specs = [hm_spec] * 3 + [kv_spec] * 2
        if kv_prev is not None:
            assert kv_prev[0].shape == kv_shape.shape
            aliases = {len(args): 3, len(args) + 1: 4}
            in_specs += [pl.BlockSpec(memory_space=pl.ANY)] * 2
            args += list(kv_prev)
    if forget:
        out_shape += [jax.ShapeDtypeStruct((m, N_HEADS), F32)]
        out_specs += [pl.BlockSpec((tm, N_HEADS), lambda i, j: (i, 0))]

    return pl.pallas_call(
        functools.partial(_qkv_kernel, forget=forget, sample=sample, q_scale=q_scale,
                          n_alias=len(aliases)),
        grid=grid,
        in_specs=in_specs,
        out_specs=out_specs,
        out_shape=out_shape,
        scratch_shapes=[pltpu.VMEM((tm, d), BF16)],
        input_output_aliases=aliases,
        compiler_params=_params("arbitrary", "arbitrary"),
        name="qkv_sample" if sample else "qkv_prompt",
    )(*args)


def _cumsum_kernel(x_ref, u_ref, o_ref, carry_ref, *, blocks_per_seq):
    @pl.when(pl.program_id(0) % blocks_per_seq == 0)
    def _():
        carry_ref[...] = jnp.zeros_like(carry_ref)

    x = x_ref[...]
    r, t = x.shape
    s = jnp.dot(_split3(x), u_ref[...], preferred_element_type=F32)
    c = s[:r] + s[r:2 * r] + s[2 * r:] + carry_ref[:, :1]
    o_ref[...] = c
    carry_ref[...] = jnp.broadcast_to(c[:, t - 1:t], carry_ref.shape)


def _cumsum_lanes(x, seq):
    r, l = x.shape
    t = min(SCAN_BLOCK, seq)
    upper = jnp.triu(jnp.ones((t, t), BF16))
    return pl.pallas_call(
        functools.partial(_cumsum_kernel, blocks_per_seq=seq // t),
        grid=(l // t,),
        in_specs=[pl.BlockSpec((r, t), lambda i: (0, i)),
                  pl.BlockSpec((t, t), lambda i: (0, 0))],
        out_specs=pl.BlockSpec((r, t), lambda i: (0, i)),
        out_shape=jax.ShapeDtypeStruct((r, l), F32),
        scratch_shapes=[pltpu.VMEM((r, LANES), F32)],
        compiler_params=_params("arbitrary"),
        name="cumsum_lanes",
    )(x, upper)


def _bias_kernel(xp_ref, xs_ref, bp_ref, bs_ref):
    def toeplitz(row, n_rows, lo, hi):
        base = jnp.broadcast_to(row, (8, ROLL_WIDTH))
        blocks = []
        for g in range(n_rows // 8):
            rolled = pltpu.roll(base, 8 * g, 1, stride=1, stride_axis=0)
            blocks.append(rolled[:, lo:hi])
        return jnp.concatenate(blocks, axis=0)

    row_c = lax.broadcasted_iota(jnp.int32, (BAND_Q, 3 * BAND_Q), 0) // CHUNK
    col = lax.broadcasted_iota(jnp.int32, (BAND_Q, 3 * BAND_Q), 1)
    col_c = (col % BAND_Q) // CHUNK
    visible = ((col >= BAND_Q) | (row_c <= col_c)) & ((col < 2 * BAND_Q) | (col_c <= row_c))
    bp_ref[0] = jnp.where(visible, toeplitz(xp_ref[0], BAND_Q, BAND_Q, ROLL_WIDTH) * LOG2E, NEG)
    bs_ref[0] = toeplitz(xs_ref[0], bs_ref.shape[1], LANES, LANES + bs_ref.shape[2])


def _rel_bias_blocks(table, n_new):
    assert 2 * BAND_Q == WINDOW and 3 * BAND_Q + BAND_Q == ROLL_WIDTH
    rev = table[:, ::-1]
    n_rel = table.shape[1]
    lead_p = 3 * BAND_Q - MAX_REL
    lead_s = WINDOW + LANES - MAX_REL
    row_p = jnp.pad(rev, ((0, 0), (lead_p, ROLL_WIDTH - n_rel - lead_p)), mode="edge")
    row_s = jnp.pad(rev, ((0, 0), (lead_s, ROLL_WIDTH - n_rel - lead_s)), mode="edge")
    h = table.shape[0]
    ws = WINDOW + LANES
    return pl.pallas_call(
        _bias_kernel,
        grid=(h,),
        in_specs=[pl.BlockSpec((1, 1, ROLL_WIDTH), lambda i: (i, 0, 0)),
                  pl.BlockSpec((1, 1, ROLL_WIDTH), lambda i: (i, 0, 0))],
        out_specs=[pl.BlockSpec((1, BAND_Q, 3 * BAND_Q), lambda i: (i, 0, 0)),
                   pl.BlockSpec((1, n_new, ws), lambda i: (i, 0, 0))],
        out_shape=[jax.ShapeDtypeStruct((h, BAND_Q, 3 * BAND_Q), F32),
                   jax.ShapeDtypeStruct((h, n_new, ws), F32)],
        compiler_params=_params("arbitrary"),
        name="rel_bias",
    )(row_p.reshape(h, 1, ROLL_WIDTH), row_s.reshape(h, 1, ROLL_WIDTH))


def _band_kernel(q_ref, k_ref, v_ref, b_ref, o_ref):
    tq = BAND_Q
    n_sub = q_ref.shape[1] // tq
    step = pl.program_id(1)
    for sub in range(n_sub):
        i = step * n_sub + sub
        q = q_ref[0, sub * tq:(sub + 1) * tq, :]
        s_parts, v_parts = [], []
        for t in range(3):
            start = pl.multiple_of(jnp.maximum(i - 2 + t, 0) * tq, tq)
            kb = k_ref[0, pl.ds(start, tq), :]
            v_parts.append(v_ref[0, pl.ds(start, tq), :])
            s = lax.dot_general(q, kb, _NT, preferred_element_type=F32)
            s = s + b_ref[0, :, t * tq:(t + 1) * tq]
            if sub + t < 2:
                s = s + jnp.where(i + t >= 2, 0.0, NEG)
            s_parts.append(s)
        s = jnp.concatenate(s_parts, axis=1)
        m = jnp.max(s, axis=-1, keepdims=True)
        p = jnp.exp2(s - m)
        l = jnp.sum(p, axis=-1, keepdims=True)
        pb = p.astype(BF16)
        acc = jnp.dot(pb[:, :tq], v_parts[0], preferred_element_type=F32)
        acc += jnp.dot(pb[:, tq:2 * tq], v_parts[1], preferred_element_type=F32)
        acc += jnp.dot(pb[:, 2 * tq:], v_parts[2], preferred_element_type=F32)
        o_ref[0, sub * tq:(sub + 1) * tq, :] = (acc / l).astype(o_ref.dtype)


def _band_attention(qb, kb, vb, bias, batch, seq):
    h, m, hd = qb.shape
    step = min(BAND_STEP, seq)
    nq = seq // step
    return pl.pallas_call(
        _band_kernel,
        grid=(h * batch, nq),
        in_specs=[pl.BlockSpec((1, step, hd), lambda g, i: (g // batch, (g % batch) * nq + i, 0)),
                  pl.BlockSpec((1, seq, hd), lambda g, i: (g // batch, g % batch, 0)),
                  pl.BlockSpec((1, seq, hd), lambda g, i: (g // batch, g % batch, 0)),
                  pl.BlockSpec((1, BAND_Q, 3 * BAND_Q), lambda g, i: (g // batch, 0, 0))],
        out_specs=pl.BlockSpec((1, step, hd), lambda g, i: (g // batch, (g % batch) * nq + i, 0)),
        out_shape=jax.ShapeDtypeStruct((h, m, hd), BF16),
        compiler_params=_params("arbitrary", "arbitrary"),
        name="band_attention",
    )(qb, kb, vb, bias)


def _band_sample_kernel(q_ref, kn_ref, vn_ref, kc_ref, vc_ref, b_ref, o_ref):
    n = q_ref.shape[0]
    lc = kc_ref.shape[0] // N_HEADS
    for h in range(N_HEADS):
        sl = slice(h * HEAD_DIM, (h + 1) * HEAD_DIM)
        rows = _head_rows(h, lc)
        q = q_ref[:, sl].astype(BF16)
        s_c = lax.dot_general(q, kc_ref[rows, :].astype(BF16), _NT, preferred_element_type=F32)
        s_n = lax.dot_general(q, kn_ref[:, sl].astype(BF16), _NT, preferred_element_type=F32)
        s_c = s_c + b_ref[h, :, :lc]
        s_n = s_n + b_ref[h, :, lc:lc + n]
        m = jnp.maximum(jnp.max(s_c, axis=-1, keepdims=True), jnp.max(s_n, axis=-1, keepdims=True))
        p_c = jnp.exp(s_c - m)
        p_n = jnp.exp(s_n - m)
        l = jnp.sum(p_c, axis=-1, keepdims=True) + jnp.sum(p_n, axis=-1, keepdims=True)
        acc = jnp.dot(p_c.astype(BF16), vc_ref[rows, :].astype(BF16), preferred_element_type=F32)
        acc += jnp.dot(p_n.astype(BF16), vn_ref[:, sl].astype(BF16), preferred_element_type=F32)
        o_ref[h] = (acc / l).astype(o_ref.dtype)


def _band_attention_sample(q, k_new, v_new, k_cache, v_cache, idx, bias, n):
    m, d = q.shape
    _, b, rows, hd = k_cache.shape
    tok = pl.BlockSpec((n, d), lambda i: (i, 0))
    cache = pl.BlockSpec((None, None, rows, hd), lambda i: (idx, i, 0, 0))
    return pl.pallas_call(
        _band_sample_kernel,
        grid=(b,),
        in_specs=[tok, tok, tok, cache, cache,
                  pl.BlockSpec(bias.shape, lambda i: (0, 0, 0))],
        out_specs=pl.BlockSpec((N_HEADS, n, HEAD_DIM), lambda i: (0, i, 0)),
        out_shape=jax.ShapeDtypeStruct((N_HEADS, m, HEAD_DIM), BF16),
        compiler_params=_params("arbitrary"),
        name="band_attention_sample",
    )(q, k_new, v_new, k_cache, v_cache, bias)


def _online_softmax_step(s, carry, v, upper=None):
    m, l, acc = carry
    m_new = jnp.maximum(m, jnp.max(s, axis=-1, keepdims=True) if upper is None else upper)
    alpha = jnp.exp2(m - m_new)
    p = jnp.exp2(s - m_new)
    l = alpha * l + jnp.sum(p, axis=-1, keepdims=True)
    acc = alpha * acc + jnp.dot(p.astype(BF16), v, preferred_element_type=F32)
    return m_new, l, acc


def _fox_kernel(q_ref, k_ref, v_ref, c_ref, *rest, n_stack):
    layer_k, layer_v = rest[:n_stack], rest[n_stack:2 * n_stack]
    o_ref = rest[2 * n_stack]
    kmax_ref = rest[-1]
    if n_stack:
        k_stack, v_stack = rest[2 * n_stack + 1:2 * n_stack + 3]
        for t in range(n_stack):
            k_stack[t] = layer_k[t][0]
            v_stack[t] = layer_v[t][0]

    i = pl.program_id(1)
    tq, hd = q_ref.shape[1], q_ref.shape[2]
    seq = k_ref.shape[1]
    blk = min(FOX_BLOCK, tq)
    tk = min(FOX_CHUNK, tq)
    n, per = tq // blk, tk // blk
    qs = [q_ref[0, a * blk:(a + 1) * blk, :] for a in range(n)]

    @pl.when(i == 0)
    def _():
        def longest(t, best):
            kk = k_ref[0, pl.ds(pl.multiple_of(t * blk, blk), blk), :].astype(F32)
            return jnp.maximum(best, jnp.sum(kk * kk, axis=-1, keepdims=True))
        best = lax.fori_loop(0, seq // blk, longest, jnp.zeros((blk, 1), F32))
        kmax_ref[...] = jnp.broadcast_to(jnp.sqrt(jnp.max(best, axis=0, keepdims=True)),
                                         kmax_ref.shape)

    kmax = kmax_ref[0:1, 0:1]
    reach = []
    for q in qs:
        qf = q.astype(F32)
        reach.append(jnp.sqrt(jnp.sum(qf * qf, axis=-1, keepdims=True)) * kmax)
    slack = 2.0 * functools.reduce(jnp.maximum, [jnp.max(r) for r in reach])

    base = pl.multiple_of(i * tq, tq)
    causal = (lax.broadcasted_iota(jnp.int32, (blk, blk), 1)
              <= lax.broadcasted_iota(jnp.int32, (blk, blk), 0))

    def c_rows(first_block, count):
        return [c_ref[0, pl.ds(first_block + t, 1), :] * LOG2E for t in range(count)]

    def c_min_blocks(first_block, count):
        mins = [jnp.min(c, axis=-1, keepdims=True) for c in c_rows(first_block, count)]
        return functools.reduce(jnp.minimum, mins)

    def logits(q, start, size, first_block):
        k = k_ref[0, pl.ds(start, size), :]
        s = lax.dot_general(q, k, _NT, preferred_element_type=F32)
        cs = c_rows(first_block, size // blk)
        parts = [s[:, t * blk:(t + 1) * blk] - cs[t] for t in range(size // blk)]
        c_min = c_min_blocks(first_block, size // blk)
        return jnp.concatenate(parts, axis=1), v_ref[0, pl.ds(start, size), :], c_min, cs[-1]

    def attend(bounded):
        def upper(a, c_min):
            return reach[a] - c_min if bounded else None

        def full_chunk(j, carry):
            start = pl.multiple_of(j * tk, tk)
            out = []
            for a in range(n):
                s, v, c_min, _ = logits(qs[a], start, tk, j * per)
                out.append(_online_softmax_step(s, carry[a], v, upper(a, c_min)))
            return tuple(out)

        init = (jnp.full((blk, 1), NEG, F32), jnp.zeros((blk, 1), F32), jnp.zeros((blk, hd), F32))
        carry = lax.fori_loop(0, i * (tq // tk), full_chunk, (init,) * n)

        for a in range(n):
            s, v, _, c_own = logits(qs[a], base, (a + 1) * blk, i * n)
            own = jnp.where(causal, s[:, a * blk:], NEG)
            s = own if a == 0 else jnp.concatenate([s[:, :a * blk], own], axis=1)
            c_vis = jnp.min(jnp.where(causal, c_own, -NEG), axis=-1, keepdims=True)
            c_seen = c_vis if a == 0 else jnp.minimum(c_vis, c_min_blocks(i * n, a))
            _, l, acc = _online_softmax_step(s, carry[a], v, upper(a, c_seen))
            o_ref[0, a * blk:(a + 1) * blk, :] = (acc / l).astype(o_ref.dtype)

    use_bound = slack <= FOX_MAX_SLACK
    pl.when(use_bound)(lambda: attend(True))
    pl.when(jnp.logical_not(use_bound))(lambda: attend(False))


def _fox_attention(qb, kb, vb, c_rows, batch, seq, stack_kv=()):
    h, m, hd = qb.shape
    t = min(FOX_STEP, seq)
    nq = seq // t
    blk = min(FOX_BLOCK, t)
    c_blocks = c_rows.reshape(h, m // blk, blk)
    in_specs = [pl.BlockSpec((1, t, hd), lambda g, i: (g // batch, (g % batch) * nq + i, 0)),
                pl.BlockSpec((1, seq, hd), lambda g, i: (g // batch, g % batch, 0)),
                pl.BlockSpec((1, seq, hd), lambda g, i: (g // batch, g % batch, 0)),
                pl.BlockSpec((1, seq // blk, blk), lambda g, i: (g // batch, g % batch, 0))]
    out_specs = [pl.BlockSpec((1, t, hd), lambda g, i: (g // batch, (g % batch) * nq + i, 0))]
    out_shape = [jax.ShapeDtypeStruct((h, m, hd), BF16)]
    n_stack = len(stack_kv)
    extra = []
    if n_stack:
        rows = stack_kv[0][0].shape[1]
        steps = h * batch * nq
        assert rows % (steps * 8) == 0
        per = rows // steps
        in_specs += [pl.BlockSpec((1, per, hd), lambda g, i: (0, g * nq + i, 0))] * (2 * n_stack)
        extra = [kv[0] for kv in stack_kv] + [kv[1] for kv in stack_kv]
        out_specs += [pl.BlockSpec((n_stack, per, hd), lambda g, i: (0, g * nq + i, 0))] * 2
        out_shape += [jax.ShapeDtypeStruct((n_stack, rows, hd), F32)] * 2
    return pl.pallas_call(
        functools.partial(_fox_kernel, n_stack=n_stack),
        grid=(h * batch, nq),
        in_specs=in_specs,
        out_specs=out_specs,
        out_shape=out_shape,
        scratch_shapes=[pltpu.VMEM((8, LANES), F32)],
        compiler_params=_params("arbitrary", "arbitrary"),
        name="fox_attention",
    )(qb, kb, vb, c_blocks, *extra)


def _fox_sample_kernel(q_ref, kn_ref, vn_ref, kc_ref, vc_ref, cc_ref, lfn_ref, u_ref, o_ref):
    n = q_ref.shape[0]
    c_cache = cc_ref[0]
    lc = c_cache.shape[1]
    s3 = jnp.dot(_split3(lfn_ref[0]), u_ref[...], preferred_element_type=F32)
    c_new = s3[:N_HEADS] + s3[N_HEADS:2 * N_HEADS] + s3[2 * N_HEADS:] + c_cache[:, lc - 1:lc]
    row = lax.broadcasted_iota(jnp.int32, (n, n), 0)
    col = lax.broadcasted_iota(jnp.int32, (n, n), 1)
    for h in range(N_HEADS):
        sl = slice(h * HEAD_DIM, (h + 1) * HEAD_DIM)
        rows = _head_rows(h, lc)
        q = q_ref[:, sl].astype(BF16)
        s_c = lax.dot_general(q, kc_ref[rows, :].astype(BF16), _NT, preferred_element_type=F32)
        s_n = lax.dot_general(q, kn_ref[:, sl].astype(BF16), _NT, preferred_element_type=F32)
        s_c = s_c - c_cache[h:h + 1, :]
        s_n = jnp.where(col <= row, s_n - c_new[h:h + 1, :], NEG)
        m = jnp.maximum(jnp.max(s_c, axis=-1, keepdims=True), jnp.max(s_n, axis=-1, keepdims=True))
        p_c = jnp.exp(s_c - m)
        p_n = jnp.exp(s_n - m)
        l = jnp.sum(p_c, axis=-1, keepdims=True) + jnp.sum(p_n, axis=-1, keepdims=True)
        acc = jnp.dot(p_c.astype(BF16), vc_ref[rows, :].astype(BF16), preferred_element_type=F32)
        acc += jnp.dot(p_n.astype(BF16), vn_ref[:, sl].astype(BF16), preferred_element_type=F32)
        o_ref[h] = (acc / l).astype(o_ref.dtype)


def _fox_attention_sample(q, k_new, v_new, k_cache, v_cache, idx, c_cache, lf_new, n):
    m, d = q.shape
    _, b, rows, hd = k_cache.shape
    lc = rows // N_HEADS
    tok = pl.BlockSpec((n, d), lambda i: (i, 0))
    cache = pl.BlockSpec((None, None, rows, hd), lambda i: (idx, i, 0, 0))
    upper = jnp.triu(jnp.ones((n, n), BF16))
    return pl.pallas_call(
        _fox_sample_kernel,
        grid=(b,),
        in_specs=[tok, tok, tok, cache, cache,
                  pl.BlockSpec((1, N_HEADS, lc), lambda i: (i, 0, 0)),
                  pl.BlockSpec((1, N_HEADS, n), lambda i: (i, 0, 0)),
                  pl.BlockSpec((n, n), lambda i: (0, 0))],
        out_specs=pl.BlockSpec((N_HEADS, n, HEAD_DIM), lambda i: (0, i, 0)),
        out_shape=jax.ShapeDtypeStruct((N_HEADS, m, HEAD_DIM), BF16),
        compiler_params=_params("arbitrary"),
        name="fox_attention_sample",
    )(q, k_new, v_new, k_cache, v_cache, c_cache, lf_new, upper)


def _out_proj_kernel(a_ref, w_ref, x_ref, o_ref):
    a = jnp.concatenate([a_ref[h] for h in range(N_HEADS)], axis=1)
    o_ref[...] = x_ref[...] + jnp.dot(a, w_ref[...], preferred_element_type=F32)


def _out_proj(attn, w_o, layer, x):
    m, d = x.shape
    tm = min(ROW_TILE, m)
    return pl.pallas_call(
        _out_proj_kernel,
        grid=(m // tm,),
        in_specs=[pl.BlockSpec((N_HEADS, tm, HEAD_DIM), lambda i: (0, i, 0)),
                  pl.BlockSpec((None, d, d), lambda i: (layer, 0, 0)),
                  pl.BlockSpec((tm, d), lambda i: (i, 0))],
        out_specs=pl.BlockSpec((tm, d), lambda i: (i, 0)),
        out_shape=jax.ShapeDtypeStruct((m, d), F32),
        compiler_params=_params("arbitrary"),
        name="out_proj",
    )(attn, w_o, x)


def _ffn_kernel(x_ref, g_ref, wg_ref, wu_ref, wd_ref, o_ref, xn_ref):
    @pl.when(pl.program_id(1) == 0)
    def _():
        x = x_ref[...]
        ms = jnp.mean(x * x, axis=-1, keepdims=True)
        xn_ref[...] = (x * lax.rsqrt(ms + EPS) * g_ref[...]).astype(BF16)
        o_ref[...] = x

    xn = xn_ref[...]
    gate = jnp.dot(xn, wg_ref[...], preferred_element_type=F32)
    up = jnp.dot(xn, wu_ref[...], preferred_element_type=F32)
    hidden = (gate * jax.nn.sigmoid(gate) * up).astype(BF16)
    o_ref[...] += jnp.dot(hidden, wd_ref[...], preferred_element_type=F32)


def _ffn(x, g_ffn, w_gate, w_up, w_down, layer):
    m, d = x.shape
    f = w_gate.shape[2]
    tm = min(FFN_ROW_TILE, m)
    tf = FFN_TILE
    return pl.pallas_call(
        _ffn_kernel,
        grid=(m // tm, f // tf),
        in_specs=[pl.BlockSpec((tm, d), lambda i, j: (i, 0)),
                  pl.BlockSpec((1, d), lambda i, j: (0, 0)),
                  pl.BlockSpec((None, d, tf), lambda i, j: (layer, 0, j)),
                  pl.BlockSpec((None, d, tf), lambda i, j: (layer, 0, j)),
                  pl.BlockSpec((None, tf, d), lambda i, j: (layer, j, 0))],
        out_specs=pl.BlockSpec((tm, d), lambda i, j: (i, 0)),
        out_shape=jax.ShapeDtypeStruct((m, d), F32),
        scratch_shapes=[pltpu.VMEM((tm, d), BF16)],
        compiler_params=_params("arbitrary", "arbitrary", vmem=FFN_VMEM_LIMIT),
        name="ffn",
    )(x, g_ffn.reshape(1, d), w_gate, w_up, w_down)


def kernel(x_prompt, x_sample, cache_a_k, cache_a_v, cache_b_k, cache_b_v, cache_b_logf,
           g_attn, w_qkv, g_q, g_k, w_o, rel_table, w_f, b_f, g_ffn, w_gate, w_up, w_down):
    batch, seq, d = x_prompt.shape
    dec_batch, n_new, _ = x_sample.shape
    depth = g_attn.shape[0]
    n_a, n_b = cache_a_k.shape[0], cache_b_k.shape[0]
    keep = min(WINDOW, seq)
    h, hd = N_HEADS, HEAD_DIM
    la, past = cache_a_k.shape[2], cache_b_k.shape[2]
    assert d == h * hd and keep == WINDOW and la == WINDOW

    w_qkv_b = w_qkv.astype(BF16)
    w_o_b = w_o.astype(BF16)
    w_f_b = w_f.astype(BF16)
    w_gate_b = w_gate.astype(BF16)
    w_up_b = w_up.astype(BF16)
    w_down_b = w_down.astype(BF16)

    cache_a_k = cache_a_k.reshape(n_a, dec_batch, la * h, hd)
    cache_a_v = cache_a_v.reshape(n_a, dec_batch, la * h, hd)
    cache_b_k = cache_b_k.reshape(n_b, dec_batch, past * h, hd)
    cache_b_v = cache_b_v.reshape(n_b, dec_batch, past * h, hd)

    yp = x_prompt.reshape(batch * seq, d)
    ys = x_sample.reshape(dec_batch * n_new, d)
    a_kv = [jnp.zeros((n_a, batch * keep * h, hd), F32) for _ in range(2)]
    b_layers, b_kv = [], None
    b_fp, a_ks, a_vs, b_ks, b_vs, b_fs = [], [], [], [], [], []

    for layer in range(depth):
        idx = layer // 2
        common = (g_attn[layer], w_qkv_b, layer, g_q[layer], g_k[layer])
        if layer % 2 == 0:
            qb, kb, vb, *a_kv = _qkv_proj(yp, *common, sample=False, q_scale=SCALE * LOG2E,
                                          kv_last=keep, seq=seq,
                                          kv_slots=n_a, kv_slot=idx, kv_prev=a_kv)
            qs, ks, vs = _qkv_proj(ys, *common, sample=True)
            bias_p, bias_s = _rel_bias_blocks(rel_table[idx], n_new)
            mp = _band_attention(qb, kb, vb, bias_p, batch, seq)
            ms = _band_attention_sample(qs, ks, vs, cache_a_k, cache_a_v, idx, bias_s, n_new)
            a_ks.append(ks)
            a_vs.append(vs)
        else:
            fw = (w_f_b[idx], b_f[idx])
            qb, kb, vb, k_all, v_all, lf = _qkv_proj(
                yp, *common, fw, sample=False, q_scale=SCALE * LOG2E, kv_slots=1)
            b_layers.append((k_all, v_all))
            qs, ks, vs, lfs = _qkv_proj(ys, *common, fw, sample=True)
            last_b = idx == n_b - 1
            mp, *b_kv = _fox_attention(qb, kb, vb, _cumsum_lanes(lf.T, seq), batch, seq,
                                       stack_kv=b_layers if last_b else ())
            lf_cache = cache_b_logf[idx].transpose(0, 2, 1).reshape(dec_batch * h, past)
            c_cache = _cumsum_lanes(lf_cache, past).reshape(dec_batch, h, past)
            lf_new = lfs.reshape(dec_batch, n_new, h).transpose(0, 2, 1)
            ms = _fox_attention_sample(qs, ks, vs, cache_b_k, cache_b_v, idx, c_cache, lf_new, n_new)
            b_fp.append(lf.reshape(batch, seq, h))
            b_ks.append(ks)
            b_vs.append(vs)
            b_fs.append(lfs.reshape(dec_batch, n_new, h))
        yp = _out_proj(mp, w_o_b, layer, yp)
        ys = _out_proj(ms, w_o_b, layer, ys)
        yp = _ffn(yp, g_ffn[layer], w_gate_b, w_up_b, w_down_b, layer)
        ys = _ffn(ys, g_ffn[layer], w_gate_b, w_up_b, w_down_b, layer)

    def new_kv(parts):
        return jnp.stack(parts).reshape(len(parts), dec_batch, n_new, h, hd)

    return (yp.reshape(batch, seq, d), ys.reshape(dec_batch, n_new, d),
            a_kv[0].reshape(n_a, batch, keep, h, hd), a_kv[1].reshape(n_a, batch, keep, h, hd),
            b_kv[0].reshape(n_b, batch, seq, h, hd), b_kv[1].reshape(n_b, batch, seq, h, hd),
            jnp.stack(b_fp), new_kv(a_ks), new_kv(a_vs), new_kv(b_ks), new_kv(b_vs), jnp.stack(b_fs))
```

```python
import functools
import math

import jax
import jax.numpy as jnp
from jax import lax
from jax.experimental import pallas as pl
from jax.experimental.pallas import tpu as pltpu

N_HEADS = 16
HEAD_DIM = 128
CHUNK = 64
N_LEFT_CHUNKS = 8
WINDOW = N_LEFT_CHUNKS * CHUNK
MAX_REL = 128
EPS = 1e-6
SCALE = HEAD_DIM ** -0.5
LOG2E = math.log2(math.e)
NEG = -1e30

BF16 = jnp.bfloat16
F32 = jnp.float32

LANES = 128
HEADS_PER_TILE = 4
ROW_TILE = 512
FFN_TILE = 512
FFN_ROW_TILE = 1024
FFN_VMEM_LIMIT = 62 * 1024 * 1024
BAND_Q = 256
BAND_STEP = 2048
FOX_BLOCK = 512
FOX_CHUNK = 2048
FOX_STEP = 2048
MAX_SLACK = 96.0
BOUND_MARGIN = 1.02
SCAN_BLOCK = 512
ROLL_WIDTH = 1024
VMEM_LIMIT = 56 * 1024 * 1024

_NT = (((1,), (1,)), ((), ()))


def _params(*sem, vmem=VMEM_LIMIT):
    return pltpu.CompilerParams(dimension_semantics=sem, vmem_limit_bytes=vmem)


def _log_sigmoid(z):
    return jnp.minimum(z, 0.0) - jnp.log1p(jnp.exp(-jnp.abs(z)))


def _qk_reach(g_q, g_k):
    return HEAD_DIM * jnp.max(jnp.abs(g_q)) * jnp.max(jnp.abs(g_k)) * (SCALE * LOG2E * BOUND_MARGIN)


def _split3(x):
    hi = x.astype(BF16).astype(F32)
    r1 = x - hi
    mid = r1.astype(BF16).astype(F32)
    lo = r1 - mid
    return jnp.concatenate([hi, mid, lo], axis=0).astype(BF16)


def _head_rows(head, n_rows):
    return pl.ds(head, n_rows, stride=N_HEADS)


def _qkv_kernel(*refs, forget, sample, q_scale, n_alias):
    x_ref, ga_ref, wq_ref, wk_ref, wv_ref, gq_ref, gk_ref = refs[:7]
    rest = list(refs[7:])
    if forget:
        wf_ref, bf_ref = rest[:2]
        rest = rest[2:]
    rest = rest[n_alias:]
    if sample:
        q_out, k_out, v_out = rest[:3]
        rest = rest[3:]
    else:
        qb_out, kb_out, vb_out, k_out, v_out = rest[:5]
        rest = rest[5:]
    if forget:
        lf_out = rest[0]
        rest = rest[1:]
    (xn_ref,) = rest
    j = pl.program_id(1)
    tm = x_ref.shape[0]

    @pl.when(j == 0)
    def _():
        x = x_ref[...]
        ms = jnp.mean(x * x, axis=-1, keepdims=True)
        xn = (x * lax.rsqrt(ms + EPS) * ga_ref[...]).astype(BF16)
        xn_ref[...] = xn
        if forget:
            z = jnp.dot(xn, wf_ref[...], preferred_element_type=F32) + bf_ref[...]
            lf_out[...] = _log_sigmoid(z)

    xn = xn_ref[...]
    q = jnp.dot(xn, wq_ref[...], preferred_element_type=F32)
    k = jnp.dot(xn, wk_ref[...], preferred_element_type=F32)
    v = jnp.dot(xn, wv_ref[...], preferred_element_type=F32)
    gq = gq_ref[...]
    gk = gk_ref[...]
    for hh in range(HEADS_PER_TILE):
        sl = slice(hh * HEAD_DIM, (hh + 1) * HEAD_DIM)
        qh = q[:, sl]
        kh = k[:, sl]
        vh = v[:, sl]
        qn = qh * lax.rsqrt(jnp.mean(qh * qh, axis=-1, keepdims=True) + EPS) * gq * q_scale
        kn = kh * lax.rsqrt(jnp.mean(kh * kh, axis=-1, keepdims=True) + EPS) * gk
        if sample:
            q_out[:, sl] = qn
            k_out[:, sl] = kn
            v_out[:, sl] = vh
        else:
            qb_out[hh] = qn.astype(BF16)
            kb_out[hh] = kn.astype(BF16)
            vb_out[hh] = vh.astype(BF16)
            rows = _head_rows(j * HEADS_PER_TILE + hh, tm)
            k_out[rows, :] = kn
            v_out[rows, :] = vh


def _qkv_proj(x, g_attn, w_qkv, layer, g_q, g_k, forget_w=None, *, sample, q_scale=SCALE,
              kv_last=None, seq=None, kv_slots=None, kv_slot=0, kv_prev=None):
    m, d = x.shape
    tm = min(ROW_TILE, m)
    tn = HEADS_PER_TILE * HEAD_DIM
    nj = d // tn
    grid = (m // tm, nj)
    forget = forget_w is not None

    in_specs = [
        pl.BlockSpec((tm, d), lambda i, j: (i, 0)),
        pl.BlockSpec((1, d), lambda i, j: (0, 0)),
        pl.BlockSpec((None, d, tn), lambda i, j: (layer, 0, j)),
        pl.BlockSpec((None, d, tn), lambda i, j: (layer, 0, nj + j)),
        pl.BlockSpec((None, d, tn), lambda i, j: (layer, 0, 2 * nj + j)),
        pl.BlockSpec((1, HEAD_DIM), lambda i, j: (0, 0)),
        pl.BlockSpec((1, HEAD_DIM), lambda i, j: (0, 0)),
    ]
    args = [x, g_attn.reshape(1, d), w_qkv, w_qkv, w_qkv,
            g_q.reshape(1, HEAD_DIM), g_k.reshape(1, HEAD_DIM)]
    if forget:
        w_f, b_f = forget_w
        in_specs += [pl.BlockSpec((d, N_HEADS), lambda i, j: (0, 0)),
                     pl.BlockSpec((1, N_HEADS), lambda i, j: (0, 0))]
        args += [w_f, b_f.reshape(1, N_HEADS)]

    aliases = {}
    if sample:
        tok_spec = pl.BlockSpec((tm, tn), lambda i, j: (i, j))
        out_shape = [jax.ShapeDtypeStruct((m, d), F32)] * 3
        out_specs = [tok_spec] * 3
    else:
        hm_shape = jax.ShapeDtypeStruct((N_HEADS, m, HEAD_DIM), BF16)
        hm_spec = pl.BlockSpec((HEADS_PER_TILE, tm, HEAD_DIM), lambda i, j: (j, i, 0))
        if kv_last is None:
            kv_rows = m
            kv_spec = pl.BlockSpec((None, tm * N_HEADS, HEAD_DIM), lambda i, j: (kv_slot, i, 0))
        else:
            assert kv_last == tm and seq % tm == 0
            per_seq = seq // tm
            kv_rows = m // seq * kv_last
            kv_spec = pl.BlockSpec((None, tm * N_HEADS, HEAD_DIM),
                                   lambda i, j: (kv_slot, i // per_seq, 0))
        kv_shape = jax.ShapeDtypeStruct((kv_slots, kv_rows * N_HEADS, HEAD_DIM), F32)
        out_shape = [hm_shape] * 3 + [kv_shape] * 2
        out_specs = [hm_spec] * 3 + [kv_spec] * 2
        if kv_prev is not None:
            assert kv_prev[0].shape == kv_shape.shape
            aliases = {len(args): 3, len(args) + 1: 4}
            in_specs += [pl.BlockSpec(memory_space=pl.ANY)] * 2
            args += list(kv_prev)
    if forget:
        out_shape += [jax.ShapeDtypeStruct((m, N_HEADS), F32)]
        out_specs += [pl.BlockSpec((tm, N_HEADS), lambda i, j: (i, 0))]

    return pl.pallas_call(
        functools.partial(_qkv_kernel, forget=forget, sample=sample, q_scale=q_scale,
                          n_alias=len(aliases)),
        grid=grid,
        in_specs=in_specs,
        out_specs=out_specs,
        out_shape=out_shape,
        scratch_shapes=[pltpu.VMEM((tm, d), BF16)],
        input_output_aliases=aliases,
        compiler_params=_params("arbitrary", "arbitrary"),
        name="qkv_sample" if sample else "qkv_prompt",
    )(*args)


def _cumsum_kernel(x_ref, u_ref, o_ref, carry_ref, *, blocks_per_seq):
    @pl.when(pl.program_id(0) % blocks_per_seq == 0)
    def _():
        carry_ref[...] = jnp.zeros_like(carry_ref)

    x = x_ref[...]
    r, t = x.shape
    s = jnp.dot(_split3(x), u_ref[...], preferred_element_type=F32)
    c = s[:r] + s[r:2 * r] + s[2 * r:] + carry_ref[:, :1]
    o_ref[...] = c
    carry_ref[...] = jnp.broadcast_to(c[:, t - 1:t], carry_ref.shape)


def _cumsum_lanes(x, seq):
    r, l = x.shape
    t = min(SCAN_BLOCK, seq)
    upper = jnp.triu(jnp.ones((t, t), BF16))
    return pl.pallas_call(
        functools.partial(_cumsum_kernel, blocks_per_seq=seq // t),
        grid=(l // t,),
        in_specs=[pl.BlockSpec((r, t), lambda i: (0, i)),
                  pl.BlockSpec((t, t), lambda i: (0, 0))],
        out_specs=pl.BlockSpec((r, t), lambda i: (0, i)),
        out_shape=jax.ShapeDtypeStruct((r, l), F32),
        scratch_shapes=[pltpu.VMEM((r, LANES), F32)],
        compiler_params=_params("arbitrary"),
        name="cumsum_lanes",
    )(x, upper)


def _bias_kernel(shift_ref, xp_ref, xs_ref, bp_ref, bs_ref):
    def toeplitz(row, n_rows, lo, hi):
        base = jnp.broadcast_to(row, (8, ROLL_WIDTH))
        blocks = []
        for g in range(n_rows // 8):
            rolled = pltpu.roll(base, 8 * g, 1, stride=1, stride_axis=0)
            blocks.append(rolled[:, lo:hi])
        return jnp.concatenate(blocks, axis=0)

    row_c = lax.broadcasted_iota(jnp.int32, (BAND_Q, 3 * BAND_Q), 0) // CHUNK
    col = lax.broadcasted_iota(jnp.int32, (BAND_Q, 3 * BAND_Q), 1)
    col_c = (col % BAND_Q) // CHUNK
    visible = ((col >= BAND_Q) | (row_c <= col_c)) & ((col < 2 * BAND_Q) | (col_c <= row_c))
    shift = shift_ref[pl.program_id(0)]
    bp_ref[0] = jnp.where(visible, toeplitz(xp_ref[0], BAND_Q, BAND_Q, ROLL_WIDTH) * LOG2E - shift, NEG)
    bs_ref[0] = toeplitz(xs_ref[0], bs_ref.shape[1], LANES, LANES + bs_ref.shape[2])


def _rel_bias_blocks(table, n_new, shift):
    assert 2 * BAND_Q == WINDOW and 3 * BAND_Q + BAND_Q == ROLL_WIDTH
    rev = table[:, ::-1]
    n_rel = table.shape[1]
    lead_p = 3 * BAND_Q - MAX_REL
    lead_s = WINDOW + LANES - MAX_REL
    row_p = jnp.pad(rev, ((0, 0), (lead_p, ROLL_WIDTH - n_rel - lead_p)), mode="edge")
    row_s = jnp.pad(rev, ((0, 0), (lead_s, ROLL_WIDTH - n_rel - lead_s)), mode="edge")
    h = table.shape[0]
    ws = WINDOW + LANES
    return pl.pallas_call(
        _bias_kernel,
        grid=(h,),
        in_specs=[pl.BlockSpec(memory_space=pltpu.SMEM),
                  pl.BlockSpec((1, 1, ROLL_WIDTH), lambda i: (i, 0, 0)),
                  pl.BlockSpec((1, 1, ROLL_WIDTH), lambda i: (i, 0, 0))],
        out_specs=[pl.BlockSpec((1, BAND_Q, 3 * BAND_Q), lambda i: (i, 0, 0)),
                   pl.BlockSpec((1, n_new, ws), lambda i: (i, 0, 0))],
        out_shape=[jax.ShapeDtypeStruct((h, BAND_Q, 3 * BAND_Q), F32),
                   jax.ShapeDtypeStruct((h, n_new, ws), F32)],
        compiler_params=_params("arbitrary"),
        name="rel_bias",
    )(shift, row_p.reshape(h, 1, ROLL_WIDTH), row_s.reshape(h, 1, ROLL_WIDTH))


def _band_kernel(shifted_ref, q_ref, k_ref, v_ref, b_ref, o_ref, *, batch):
    tq = BAND_Q
    n_sub = q_ref.shape[1] // tq
    step = pl.program_id(1)

    def attend(shifted):
        for sub in range(n_sub):
            i = step * n_sub + sub
            q = q_ref[0, sub * tq:(sub + 1) * tq, :]
            s_parts, v_parts = [], []
            for t in range(3):
                start = pl.multiple_of(jnp.maximum(i - 2 + t, 0) * tq, tq)
                kb = k_ref[0, pl.ds(start, tq), :]
                v_parts.append(v_ref[0, pl.ds(start, tq), :])
                s = lax.dot_general(q, kb, _NT, preferred_element_type=F32)
                s = s + b_ref[0, :, t * tq:(t + 1) * tq]
                if sub + t < 2:
                    s = s + jnp.where(i + t >= 2, 0.0, NEG)
                s_parts.append(s)
            s = jnp.concatenate(s_parts, axis=1)
            if not shifted:
                s = s - jnp.max(s, axis=-1, keepdims=True)
            p = jnp.exp2(s)
            l = jnp.sum(p, axis=-1, keepdims=True)
            pb = p.astype(BF16)
            acc = jnp.dot(pb[:, :tq], v_parts[0], preferred_element_type=F32)
            acc += jnp.dot(pb[:, tq:2 * tq], v_parts[1], preferred_element_type=F32)
            acc += jnp.dot(pb[:, 2 * tq:], v_parts[2], preferred_element_type=F32)
            o_ref[0, sub * tq:(sub + 1) * tq, :] = (acc / l).astype(o_ref.dtype)

    shifted = shifted_ref[pl.program_id(0) // batch] != 0
    pl.when(shifted)(lambda: attend(True))
    pl.when(jnp.logical_not(shifted))(lambda: attend(False))


def _band_attention(qb, kb, vb, bias, shifted, batch, seq):
    h, m, hd = qb.shape
    step = min(BAND_STEP, seq)
    nq = seq // step
    return pl.pallas_call(
        functools.partial(_band_kernel, batch=batch),
        grid=(h * batch, nq),
        in_specs=[pl.BlockSpec(memory_space=pltpu.SMEM),
                  pl.BlockSpec((1, step, hd), lambda g, i: (g // batch, (g % batch) * nq + i, 0)),
                  pl.BlockSpec((1, seq, hd), lambda g, i: (g // batch, g % batch, 0)),
                  pl.BlockSpec((1, seq, hd), lambda g, i: (g // batch, g % batch, 0)),
                  pl.BlockSpec((1, BAND_Q, 3 * BAND_Q), lambda g, i: (g // batch, 0, 0))],
        out_specs=pl.BlockSpec((1, step, hd), lambda g, i: (g // batch, (g % batch) * nq + i, 0)),
        out_shape=jax.ShapeDtypeStruct((h, m, hd), BF16),
        compiler_params=_params("arbitrary", "arbitrary"),
        name="band_attention",
    )(shifted, qb, kb, vb, bias)


def _band_sample_kernel(q_ref, kn_ref, vn_ref, kc_ref, vc_ref, b_ref, o_ref):
    n = q_ref.shape[0]
    lc = kc_ref.shape[0] // N_HEADS
    for h in range(N_HEADS):
        sl = slice(h * HEAD_DIM, (h + 1) * HEAD_DIM)
        rows = _head_rows(h, lc)
        q = q_ref[:, sl].astype(BF16)
        s_c = lax.dot_general(q, kc_ref[rows, :].astype(BF16), _NT, preferred_element_type=F32)
        s_n = lax.dot_general(q, kn_ref[:, sl].astype(BF16), _NT, preferred_element_type=F32)
        s_c = s_c + b_ref[h, :, :lc]
        s_n = s_n + b_ref[h, :, lc:lc + n]
        m = jnp.maximum(jnp.max(s_c, axis=-1, keepdims=True), jnp.max(s_n, axis=-1, keepdims=True))
        p_c = jnp.exp(s_c - m)
        p_n = jnp.exp(s_n - m)
        l = jnp.sum(p_c, axis=-1, keepdims=True) + jnp.sum(p_n, axis=-1, keepdims=True)
        acc = jnp.dot(p_c.astype(BF16), vc_ref[rows, :].astype(BF16), preferred_element_type=F32)
        acc += jnp.dot(p_n.astype(BF16), vn_ref[:, sl].astype(BF16), preferred_element_type=F32)
        o_ref[h] = (acc / l).astype(o_ref.dtype)


def _band_attention_sample(q, k_new, v_new, k_cache, v_cache, idx, bias, n):
    m, d = q.shape
    _, b, rows, hd = k_cache.shape
    tok = pl.BlockSpec((n, d), lambda i: (i, 0))
    cache = pl.BlockSpec((None, None, rows, hd), lambda i: (idx, i, 0, 0))
    return pl.pallas_call(
        _band_sample_kernel,
        grid=(b,),
        in_specs=[tok, tok, tok, cache, cache,
                  pl.BlockSpec(bias.shape, lambda i: (0, 0, 0))],
        out_specs=pl.BlockSpec((N_HEADS, n, HEAD_DIM), lambda i: (0, i, 0)),
        out_shape=jax.ShapeDtypeStruct((N_HEADS, m, HEAD_DIM), BF16),
        compiler_params=_params("arbitrary"),
        name="band_attention_sample",
    )(q, k_new, v_new, k_cache, v_cache, bias)


def _online_softmax_step(s, carry, v):
    m, l, acc = carry
    m_new = jnp.maximum(m, jnp.max(s, axis=-1, keepdims=True))
    alpha = jnp.exp2(m - m_new)
    p = jnp.exp2(s - m_new)
    l = alpha * l + jnp.sum(p, axis=-1, keepdims=True)
    acc = alpha * acc + jnp.dot(p.astype(BF16), v, preferred_element_type=F32)
    return m_new, l, acc


def _fox_kernel(reach_ref, q_ref, k_ref, v_ref, c_ref, *rest, n_stack):
    layer_k, layer_v = rest[:n_stack], rest[n_stack:2 * n_stack]
    o_ref = rest[2 * n_stack]
    if n_stack:
        k_stack, v_stack = rest[2 * n_stack + 1:2 * n_stack + 3]
        for t in range(n_stack):
            k_stack[t] = layer_k[t][0]
            v_stack[t] = layer_v[t][0]

    i = pl.program_id(1)
    tq, hd = q_ref.shape[1], q_ref.shape[2]
    blk = min(FOX_BLOCK, tq)
    tk = min(FOX_CHUNK, tq)
    n, per = tq // blk, tk // blk
    qs = [q_ref[0, a * blk:(a + 1) * blk, :] for a in range(n)]
    reach = reach_ref[0]
    base = pl.multiple_of(i * tq, tq)
    causal = (lax.broadcasted_iota(jnp.int32, (blk, blk), 1)
              <= lax.broadcasted_iota(jnp.int32, (blk, blk), 0))

    def c_rows(first_block, count):
        return [c_ref[0, pl.ds(first_block + t, 1), :] * LOG2E for t in range(count)]

    def c_min_blocks(first_block, count):
        mins = [jnp.min(c, axis=-1, keepdims=True) for c in c_rows(first_block, count)]
        return functools.reduce(jnp.minimum, mins)

    def scores(q, start, size):
        k = k_ref[0, pl.ds(start, size), :]
        return (lax.dot_general(q, k, _NT, preferred_element_type=F32),
                v_ref[0, pl.ds(start, size), :])

    def minus_rows(s, rows):
        return jnp.concatenate([s[:, t * blk:(t + 1) * blk] - r for t, r in enumerate(rows)], axis=1)

    def mask_own(s, a):
        own = jnp.where(causal, s[:, a * blk:], NEG)
        return own if a == 0 else jnp.concatenate([s[:, :a * blk], own], axis=1)

    def finish(a, l, acc):
        o_ref[0, a * blk:(a + 1) * blk, :] = (acc / l).astype(o_ref.dtype)

    def attend_exact():
        def full_chunk(j, carry):
            start = pl.multiple_of(j * tk, tk)
            out = []
            for a in range(n):
                s, v = scores(qs[a], start, tk)
                out.append(_online_softmax_step(minus_rows(s, c_rows(j * per, per)), carry[a], v))
            return tuple(out)

        init = (jnp.full((blk, 1), NEG, F32), jnp.zeros((blk, 1), F32), jnp.zeros((blk, hd), F32))
        carry = lax.fori_loop(0, i * (tq // tk), full_chunk, (init,) * n)
        for a in range(n):
            s, v = scores(qs[a], base, (a + 1) * blk)
            s = mask_own(minus_rows(s, c_rows(i * n, a + 1)), a)
            _, l, acc = _online_softmax_step(s, carry[a], v)
            finish(a, l, acc)

    def attend_bounded():
        def update(s, m_new, carry, v):
            m, l, acc = carry
            alpha = jnp.exp2(m - m_new)
            p = jnp.exp2(s)
            l = alpha * l + jnp.sum(p, axis=-1, keepdims=True)
            return m_new, l, alpha * acc + jnp.dot(p.astype(BF16), v, preferred_element_type=F32)

        def full_chunk(j, carry):
            start = pl.multiple_of(j * tk, tk)
            rows = c_rows(j * per, per)
            m_new = jnp.maximum(carry[0][0], reach - c_min_blocks(j * per, per))
            shifted = [r + m_new for r in rows]
            out = []
            for a in range(n):
                s, v = scores(qs[a], start, tk)
                out.append(update(minus_rows(s, shifted), m_new, carry[a], v))
            return tuple(out)

        init = (jnp.full((1, 1), NEG, F32), jnp.zeros((blk, 1), F32), jnp.zeros((blk, hd), F32))
        carry = lax.fori_loop(0, i * (tq // tk), full_chunk, (init,) * n)
        for a in range(n):
            rows = c_rows(i * n, a + 1)
            c_vis = jnp.min(jnp.where(causal, rows[-1], -NEG), axis=-1, keepdims=True)
            c_seen = c_vis if a == 0 else jnp.minimum(c_vis, c_min_blocks(i * n, a))
            m_new = jnp.maximum(carry[a][0], reach - c_seen)
            s, v = scores(qs[a], base, (a + 1) * blk)
            s = mask_own(minus_rows(s, rows) - m_new, a)
            _, l, acc = update(s, m_new, carry[a], v)
            finish(a, l, acc)

    use_bound = 2.0 * reach <= MAX_SLACK
    pl.when(use_bound)(attend_bounded)
    pl.when(jnp.logical_not(use_bound))(attend_exact)


def _fox_attention(qb, kb, vb, c_rows, reach, batch, seq, stack_kv=()):
    h, m, hd = qb.shape
    t = min(FOX_STEP, seq)
    nq = seq // t
    blk = min(FOX_BLOCK, t)
    c_blocks = c_rows.reshape(h, m // blk, blk)
    in_specs = [pl.BlockSpec(memory_space=pltpu.SMEM),
                pl.BlockSpec((1, t, hd), lambda g, i: (g // batch, (g % batch) * nq + i, 0)),
                pl.BlockSpec((1, seq, hd), lambda g, i: (g // batch, g % batch, 0)),
                pl.BlockSpec((1, seq, hd), lambda g, i: (g // batch, g % batch, 0)),
                pl.BlockSpec((1, seq // blk, blk), lambda g, i: (g // batch, g % batch, 0))]
    out_specs = [pl.BlockSpec((1, t, hd), lambda g, i: (g // batch, (g % batch) * nq + i, 0))]
    out_shape = [jax.ShapeDtypeStruct((h, m, hd), BF16)]
    n_stack = len(stack_kv)
    extra = []
    if n_stack:
        rows = stack_kv[0][0].shape[1]
        steps = h * batch * nq
        assert rows % (steps * 8) == 0
        per = rows // steps
        in_specs += [pl.BlockSpec((1, per, hd), lambda g, i: (0, g * nq + i, 0))] * (2 * n_stack)
        extra = [kv[0] for kv in stack_kv] + [kv[1] for kv in stack_kv]
        out_specs += [pl.BlockSpec((n_stack, per, hd), lambda g, i: (0, g * nq + i, 0))] * 2
        out_shape += [jax.ShapeDtypeStruct((n_stack, rows, hd), F32)] * 2
    return pl.pallas_call(
        functools.partial(_fox_kernel, n_stack=n_stack),
        grid=(h * batch, nq),
        in_specs=in_specs,
        out_specs=out_specs,
        out_shape=out_shape,
        compiler_params=_params("arbitrary", "arbitrary"),
        name="fox_attention",
    )(reach.reshape(1), qb, kb, vb, c_blocks, *extra)


def _fox_sample_kernel(q_ref, kn_ref, vn_ref, kc_ref, vc_ref, cc_ref, lfn_ref, u_ref, o_ref):
    n = q_ref.shape[0]
    c_cache = cc_ref[0]
    lc = c_cache.shape[1]
    s3 = jnp.dot(_split3(lfn_ref[0]), u_ref[...], preferred_element_type=F32)
    c_new = s3[:N_HEADS] + s3[N_HEADS:2 * N_HEADS] + s3[2 * N_HEADS:] + c_cache[:, lc - 1:lc]
    row = lax.broadcasted_iota(jnp.int32, (n, n), 0)
    col = lax.broadcasted_iota(jnp.int32, (n, n), 1)
    for h in range(N_HEADS):
        sl = slice(h * HEAD_DIM, (h + 1) * HEAD_DIM)
        rows = _head_rows(h, lc)
        q = q_ref[:, sl].astype(BF16)
        s_c = lax.dot_general(q, kc_ref[rows, :].astype(BF16), _NT, preferred_element_type=F32)
        s_n = lax.dot_general(q, kn_ref[:, sl].astype(BF16), _NT, preferred_element_type=F32)
        s_c = s_c - c_cache[h:h + 1, :]
        s_n = jnp.where(col <= row, s_n - c_new[h:h + 1, :], NEG)
        m = jnp.maximum(jnp.max(s_c, axis=-1, keepdims=True), jnp.max(s_n, axis=-1, keepdims=True))
        p_c = jnp.exp(s_c - m)
        p_n = jnp.exp(s_n - m)
        l = jnp.sum(p_c, axis=-1, keepdims=True) + jnp.sum(p_n, axis=-1, keepdims=True)
        acc = jnp.dot(p_c.astype(BF16), vc_ref[rows, :].astype(BF16), preferred_element_type=F32)
        acc += jnp.dot(p_n.astype(BF16), vn_ref[:, sl].astype(BF16), preferred_element_type=F32)
        o_ref[h] = (acc / l).astype(o_ref.dtype)


def _fox_attention_sample(q, k_new, v_new, k_cache, v_cache, idx, c_cache, lf_new, n):
    m, d = q.shape
    _, b, rows, hd = k_cache.shape
    lc = rows // N_HEADS
    tok = pl.BlockSpec((n, d), lambda i: (i, 0))
    cache = pl.BlockSpec((None, None, rows, hd), lambda i: (idx, i, 0, 0))
    upper = jnp.triu(jnp.ones((n, n), BF16))
    return pl.pallas_call(
        _fox_sample_kernel,
        grid=(b,),
        in_specs=[tok, tok, tok, cache, cache,
                  pl.BlockSpec((1, N_HEADS, lc), lambda i: (i, 0, 0)),
                  pl.BlockSpec((1, N_HEADS, n), lambda i: (i, 0, 0)),
                  pl.BlockSpec((n, n), lambda i: (0, 0))],
        out_specs=pl.BlockSpec((N_HEADS, n, HEAD_DIM), lambda i: (0, i, 0)),
        out_shape=jax.ShapeDtypeStruct((N_HEADS, m, HEAD_DIM), BF16),
        compiler_params=_params("arbitrary"),
        name="fox_attention_sample",
    )(q, k_new, v_new, k_cache, v_cache, c_cache, lf_new, upper)


def _out_proj_kernel(a_ref, w_ref, x_ref, o_ref):
    a = jnp.concatenate([a_ref[h] for h in range(N_HEADS)], axis=1)
    o_ref[...] = x_ref[...] + jnp.dot(a, w_ref[...], preferred_element_type=F32)


def _out_proj(attn, w_o, layer, x):
    m, d = x.shape
    tm = min(ROW_TILE, m)
    return pl.pallas_call(
        _out_proj_kernel,
        grid=(m // tm,),
        in_specs=[pl.BlockSpec((N_HEADS, tm, HEAD_DIM), lambda i: (0, i, 0)),
                  pl.BlockSpec((None, d, d), lambda i: (layer, 0, 0)),
                  pl.BlockSpec((tm, d), lambda i: (i, 0))],
        out_specs=pl.BlockSpec((tm, d), lambda i: (i, 0)),
        out_shape=jax.ShapeDtypeStruct((m, d), F32),
        compiler_params=_params("arbitrary"),
        name="out_proj",
    )(attn, w_o, x)


def _ffn_kernel(x_ref, g_ref, wg_ref, wu_ref, wd_ref, o_ref, xn_ref):
    @pl.when(pl.program_id(1) == 0)
    def _():
        x = x_ref[...]
        ms = jnp.mean(x * x, axis=-1, keepdims=True)
        xn_ref[...] = (x * lax.rsqrt(ms + EPS) * g_ref[...]).astype(BF16)
        o_ref[...] = x

    xn = xn_ref[...]
    gate = jnp.dot(xn, wg_ref[...], preferred_element_type=F32)
    up = jnp.dot(xn, wu_ref[...], preferred_element_type=F32)
    hidden = (gate * jax.nn.sigmoid(gate) * up).astype(BF16)
    o_ref[...] += jnp.dot(hidden, wd_ref[...], preferred_element_type=F32)


def _ffn(x, g_ffn, w_gate, w_up, w_down, layer):
    m, d = x.shape
    f = w_gate.shape[2]
    tm = min(FFN_ROW_TILE, m)
    tf = FFN_TILE
    return pl.pallas_call(
        _ffn_kernel,
        grid=(m // tm, f // tf),
        in_specs=[pl.BlockSpec((tm, d), lambda i, j: (i, 0)),
                  pl.BlockSpec((1, d), lambda i, j: (0, 0)),
                  pl.BlockSpec((None, d, tf), lambda i, j: (layer, 0, j)),
                  pl.BlockSpec((None, d, tf), lambda i, j: (layer, 0, j)),
                  pl.BlockSpec((None, tf, d), lambda i, j: (layer, j, 0))],
        out_specs=pl.BlockSpec((tm, d), lambda i, j: (i, 0)),
        out_shape=jax.ShapeDtypeStruct((m, d), F32),
        scratch_shapes=[pltpu.VMEM((tm, d), BF16)],
        compiler_params=_params("arbitrary", "arbitrary", vmem=FFN_VMEM_LIMIT),
        name="ffn",
    )(x, g_ffn.reshape(1, d), w_gate, w_up, w_down)


def kernel(x_prompt, x_sample, cache_a_k, cache_a_v, cache_b_k, cache_b_v, cache_b_logf,
           g_attn, w_qkv, g_q, g_k, w_o, rel_table, w_f, b_f, g_ffn, w_gate, w_up, w_down):
    batch, seq, d = x_prompt.shape
    dec_batch, n_new, _ = x_sample.shape
    depth = g_attn.shape[0]
    n_a, n_b = cache_a_k.shape[0], cache_b_k.shape[0]
    keep = min(WINDOW, seq)
    h, hd = N_HEADS, HEAD_DIM
    la, past = cache_a_k.shape[2], cache_b_k.shape[2]
    assert d == h * hd and keep == WINDOW and la == WINDOW

    w_qkv_b = w_qkv.astype(BF16)
    w_o_b = w_o.astype(BF16)
    w_f_b = w_f.astype(BF16)
    w_gate_b = w_gate.astype(BF16)
    w_up_b = w_up.astype(BF16)
    w_down_b = w_down.astype(BF16)

    cache_a_k = cache_a_k.reshape(n_a, dec_batch, la * h, hd)
    cache_a_v = cache_a_v.reshape(n_a, dec_batch, la * h, hd)
    cache_b_k = cache_b_k.reshape(n_b, dec_batch, past * h, hd)
    cache_b_v = cache_b_v.reshape(n_b, dec_batch, past * h, hd)

    yp = x_prompt.reshape(batch * seq, d)
    ys = x_sample.reshape(dec_batch * n_new, d)
    a_kv = [jnp.zeros((n_a, batch * keep * h, hd), F32) for _ in range(2)]
    b_layers, b_kv = [], None
    b_fp, a_ks, a_vs, b_ks, b_vs, b_fs = [], [], [], [], [], []

    for layer in range(depth):
        idx = layer // 2
        common = (g_attn[layer], w_qkv_b, layer, g_q[layer], g_k[layer])
        reach = _qk_reach(g_q[layer], g_k[layer])
        if layer % 2 == 0:
            qb, kb, vb, *a_kv = _qkv_proj(yp, *common, sample=False, q_scale=SCALE * LOG2E,
                                          kv_last=keep, seq=seq,
                                          kv_slots=n_a, kv_slot=idx, kv_prev=a_kv)
            qs, ks, vs = _qkv_proj(ys, *common, sample=True)
            b_hi = jnp.max(rel_table[idx], axis=1) * LOG2E
            b_lo = jnp.min(rel_table[idx], axis=1) * LOG2E
            shifted = 2.0 * reach + (b_hi - b_lo) <= MAX_SLACK
            bias_p, bias_s = _rel_bias_blocks(rel_table[idx], n_new,
                                              jnp.where(shifted, reach + b_hi, 0.0))
            mp = _band_attention(qb, kb, vb, bias_p, shifted.astype(jnp.int32), batch, seq)
            ms = _band_attention_sample(qs, ks, vs, cache_a_k, cache_a_v, idx, bias_s, n_new)
            a_ks.append(ks)
            a_vs.append(vs)
        else:
            fw = (w_f_b[idx], b_f[idx])
            qb, kb, vb, k_all, v_all, lf = _qkv_proj(
                yp, *common, fw, sample=False, q_scale=SCALE * LOG2E, kv_slots=1)
            b_layers.append((k_all, v_all))
            qs, ks, vs, lfs = _qkv_proj(ys, *common, fw, sample=True)
            last_b = idx == n_b - 1
            mp, *b_kv = _fox_attention(qb, kb, vb, _cumsum_lanes(lf.T, seq), reach, batch, seq,
                                       stack_kv=b_layers if last_b else ())
            lf_cache = cache_b_logf[idx].transpose(0, 2, 1).reshape(dec_batch * h, past)
            c_cache = _cumsum_lanes(lf_cache, past).reshape(dec_batch, h, past)
            lf_new = lfs.reshape(dec_batch, n_new, h).transpose(0, 2, 1)
            ms = _fox_attention_sample(qs, ks, vs, cache_b_k, cache_b_v, idx, c_cache, lf_new, n_new)
            b_fp.append(lf.reshape(batch, seq, h))
            b_ks.append(ks)
            b_vs.append(vs)
            b_fs.append(lfs.reshape(dec_batch, n_new, h))
        yp = _out_proj(mp, w_o_b, layer, yp)
        ys = _out_proj(ms, w_o_b, layer, ys)
        yp = _ffn(yp, g_ffn[layer], w_gate_b, w_up_b, w_down_b, layer)
        ys = _ffn(ys, g_ffn[layer], w_gate_b, w_up_b, w_down_b, layer)

    def new_kv(parts):
        return jnp.stack(parts).reshape(len(parts), dec_batch, n_new, h, hd)

    return (yp.reshape(batch, seq, d), ys.reshape(dec_batch, n_new, d),
            a_kv[0].reshape(n_a, batch, keep, h, hd), a_kv[1].reshape(n_a, batch, keep, h, hd),
            b_kv[0].reshape(n_b, batch, seq, h, hd), b_kv[1].reshape(n_b, batch, seq, h, hd),
            jnp.stack(b_fp), new_kv(a_ks), new_kv(a_vs), new_kv(b_ks), new_kv(b_vs), jnp.stack(b_fs))
```

```python
import functools
import math

import jax
import jax.numpy as jnp
from jax import lax
from jax.experimental import pallas as pl
from jax.experimental.pallas import tpu as pltpu

N_HEADS = 16
HEAD_DIM = 128
CHUNK = 64
N_LEFT_CHUNKS = 8
WINDOW = N_LEFT_CHUNKS * CHUNK
MAX_REL = 128
EPS = 1e-6
SCALE = HEAD_DIM ** -0.5
LOG2E = math.log2(math.e)
NEG = -1e30

BF16 = jnp.bfloat16
F32 = jnp.float32

LANES = 128
HEADS_PER_TILE = 4
ROW_TILE = 512
FFN_TILE = 512
FFN_ROW_TILE = 1024
FFN_VMEM_LIMIT = 62 * 1024 * 1024
BAND_Q = 256
BAND_STEP = 2048
FOX_BLOCK = 512
FOX_CHUNK = 2048
FOX_STEP = 2048
MAX_SLACK = 96.0
BOUND_MARGIN = 1.02
SCAN_BLOCK = 512
ROLL_WIDTH = 1024
VMEM_LIMIT = 56 * 1024 * 1024

_NT = (((1,), (1,)), ((), ()))


def _params(*sem, vmem=VMEM_LIMIT):
    return pltpu.CompilerParams(dimension_semantics=sem, vmem_limit_bytes=vmem)


def _log_sigmoid(z):
    return jnp.minimum(z, 0.0) - jnp.log1p(jnp.exp(-jnp.abs(z)))


def _qk_reach(g_q, g_k):
    return HEAD_DIM * jnp.max(jnp.abs(g_q)) * jnp.max(jnp.abs(g_k)) * (SCALE * LOG2E * BOUND_MARGIN)


def _split3(x):
    hi = x.astype(BF16).astype(F32)
    r1 = x - hi
    mid = r1.astype(BF16).astype(F32)
    lo = r1 - mid
    return jnp.concatenate([hi, mid, lo], axis=0).astype(BF16)


def _head_rows(head, n_rows):
    return pl.ds(head, n_rows, stride=N_HEADS)


def _qkv_kernel(*refs, forget, sample, q_scale, n_alias, n_cast):
    x_ref, ga_ref, wq_ref, wk_ref, wv_ref, gq_ref, gk_ref = refs[:7]
    rest = list(refs[7:])
    if forget:
        wf_ref, bf_ref = rest[:2]
        rest = rest[2:]
    rest = rest[n_alias:]
    cast_in, rest = rest[:n_cast], rest[n_cast:]
    if sample:
        q_out, k_out, v_out = rest[:3]
        rest = rest[3:]
    else:
        qb_out, kb_out, vb_out, k_out, v_out = rest[:5]
        rest = rest[5:]
    if forget:
        lf_out = rest[0]
        rest = rest[1:]
    cast_out, rest = rest[:n_cast], rest[n_cast:]
    (xn_ref,) = rest
    j = pl.program_id(1)
    tm = x_ref.shape[0]

    for src, dst in zip(cast_in, cast_out):
        dst[...] = src[...].astype(BF16)

    @pl.when(j == 0)
    def _():
        x = x_ref[...]
        ms = jnp.mean(x * x, axis=-1, keepdims=True)
        xn = (x * lax.rsqrt(ms + EPS) * ga_ref[...]).astype(BF16)
        xn_ref[...] = xn
        if forget:
            z = jnp.dot(xn, wf_ref[...], preferred_element_type=F32) + bf_ref[...]
            lf_out[...] = _log_sigmoid(z)

    xn = xn_ref[...]
    q = jnp.dot(xn, wq_ref[...], preferred_element_type=F32)
    k = jnp.dot(xn, wk_ref[...], preferred_element_type=F32)
    v = jnp.dot(xn, wv_ref[...], preferred_element_type=F32)
    gq = gq_ref[...]
    gk = gk_ref[...]
    for hh in range(HEADS_PER_TILE):
        sl = slice(hh * HEAD_DIM, (hh + 1) * HEAD_DIM)
        qh = q[:, sl]
        kh = k[:, sl]
        vh = v[:, sl]
        qn = qh * lax.rsqrt(jnp.mean(qh * qh, axis=-1, keepdims=True) + EPS) * gq * q_scale
        kn = kh * lax.rsqrt(jnp.mean(kh * kh, axis=-1, keepdims=True) + EPS) * gk
        if sample:
            q_out[:, sl] = qn
            k_out[:, sl] = kn
            v_out[:, sl] = vh
        else:
            qb_out[hh] = qn.astype(BF16)
            kb_out[hh] = kn.astype(BF16)
            vb_out[hh] = vh.astype(BF16)
            rows = _head_rows(j * HEADS_PER_TILE + hh, tm)
            k_out[rows, :] = kn
            v_out[rows, :] = vh


def _cast_rows(rows, steps):
    group = 1
    while rows * group % (steps * 16):
        group *= 2
        assert steps % group == 0
    return rows * group // steps, group


def _qkv_proj(x, g_attn, w_qkv, g_q, g_k, forget_w=None, *, sample, q_scale=SCALE,
              kv_last=None, seq=None, kv_slots=None, kv_slot=0, kv_prev=None, cast=()):
    m, d = x.shape
    tm = min(ROW_TILE, m)
    tn = HEADS_PER_TILE * HEAD_DIM
    nj = d // tn
    grid = (m // tm, nj)
    forget = forget_w is not None

    in_specs = [
        pl.BlockSpec((tm, d), lambda i, j: (i, 0)),
        pl.BlockSpec((1, d), lambda i, j: (0, 0)),
        pl.BlockSpec((d, tn), lambda i, j: (0, j)),
        pl.BlockSpec((d, tn), lambda i, j: (0, nj + j)),
        pl.BlockSpec((d, tn), lambda i, j: (0, 2 * nj + j)),
        pl.BlockSpec((1, HEAD_DIM), lambda i, j: (0, 0)),
        pl.BlockSpec((1, HEAD_DIM), lambda i, j: (0, 0)),
    ]
    args = [x, g_attn.reshape(1, d), w_qkv, w_qkv, w_qkv,
            g_q.reshape(1, HEAD_DIM), g_k.reshape(1, HEAD_DIM)]
    if forget:
        w_f, b_f = forget_w
        in_specs += [pl.BlockSpec((d, N_HEADS), lambda i, j: (0, 0)),
                     pl.BlockSpec((1, N_HEADS), lambda i, j: (0, 0))]
        args += [w_f, b_f.reshape(1, N_HEADS)]

    aliases = {}
    if sample:
        tok_spec = pl.BlockSpec((tm, tn), lambda i, j: (i, j))
        out_shape = [jax.ShapeDtypeStruct((m, d), F32)] * 3
        out_specs = [tok_spec] * 3
    else:
        hm_shape = jax.ShapeDtypeStruct((N_HEADS, m, HEAD_DIM), BF16)
        hm_spec = pl.BlockSpec((HEADS_PER_TILE, tm, HEAD_DIM), lambda i, j: (j, i, 0))
        if kv_last is None:
            kv_rows = m
            kv_spec = pl.BlockSpec((None, tm * N_HEADS, HEAD_DIM), lambda i, j: (kv_slot, i, 0))
        else:
            assert kv_last == tm and seq % tm == 0
            per_seq = seq // tm
            kv_rows = m // seq * kv_last
            kv_spec = pl.BlockSpec((None, tm * N_HEADS, HEAD_DIM),
                                   lambda i, j: (kv_slot, i // per_seq, 0))
        kv_shape = jax.ShapeDtypeStruct((kv_slots, kv_rows * N_HEADS, HEAD_DIM), F32)
        out_shape = [hm_shape] * 3 + [kv_shape] * 2
        out_specs = [hm_spec] * 3 + [kv_spec] * 2
        if kv_prev is not None:
            assert kv_prev[0].shape == kv_shape.shape
            aliases = {len(args): 3, len(args) + 1: 4}
            in_specs += [pl.BlockSpec(memory_space=pl.ANY)] * 2
            args += list(kv_prev)
    if forget:
        out_shape += [jax.ShapeDtypeStruct((m, N_HEADS), F32)]
        out_specs += [pl.BlockSpec((tm, N_HEADS), lambda i, j: (i, 0))]
    n_main = len(out_shape)
    for stack, layer in cast:
        _, rows, cols = stack.shape
        rb, group = _cast_rows(rows, grid[0] * nj)
        in_specs.append(pl.BlockSpec(
            (None, rb, cols), lambda i, j, layer=layer, group=group: (layer, (i * nj + j) // group, 0)))
        args.append(stack)
        out_specs.append(pl.BlockSpec((rb, cols), lambda i, j, group=group: ((i * nj + j) // group, 0)))
        out_shape.append(jax.ShapeDtypeStruct((rows, cols), BF16))

    outs = pl.pallas_call(
        functools.partial(_qkv_kernel, forget=forget, sample=sample, q_scale=q_scale,
                          n_alias=len(aliases), n_cast=len(cast)),
        grid=grid,
        in_specs=in_specs,
        out_specs=out_specs,
        out_shape=out_shape,
        scratch_shapes=[pltpu.VMEM((tm, d), BF16)],
        input_output_aliases=aliases,
        compiler_params=_params("arbitrary", "arbitrary"),
        name="qkv_sample" if sample else "qkv_prompt",
    )(*args)
    return outs[:n_main], outs[n_main:]


def _cumsum_kernel(x_ref, u_ref, o_ref, carry_ref, *, blocks_per_seq):
    @pl.when(pl.program_id(0) % blocks_per_seq == 0)
    def _():
        carry_ref[...] = jnp.zeros_like(carry_ref)

    x = x_ref[...]
    r, t = x.shape
    s = jnp.dot(_split3(x), u_ref[...], preferred_element_type=F32)
    c = s[:r] + s[r:2 * r] + s[2 * r:] + carry_ref[:, :1]
    o_ref[...] = c
    carry_ref[...] = jnp.broadcast_to(c[:, t - 1:t], carry_ref.shape)


def _cumsum_lanes(x, seq):
    r, l = x.shape
    t = min(SCAN_BLOCK, seq)
    upper = jnp.triu(jnp.ones((t, t), BF16))
    return pl.pallas_call(
        functools.partial(_cumsum_kernel, blocks_per_seq=seq // t),
        grid=(l // t,),
        in_specs=[pl.BlockSpec((r, t), lambda i: (0, i)),
                  pl.BlockSpec((t, t), lambda i: (0, 0))],
        out_specs=pl.BlockSpec((r, t), lambda i: (0, i)),
        out_shape=jax.ShapeDtypeStruct((r, l), F32),
        scratch_shapes=[pltpu.VMEM((r, LANES), F32)],
        compiler_params=_params("arbitrary"),
        name="cumsum_lanes",
    )(x, upper)


def _bias_kernel(shift_ref, xp_ref, xs_ref, bp_ref, bs_ref):
    def toeplitz(row, n_rows, lo, hi):
        base = jnp.broadcast_to(row, (8, ROLL_WIDTH))
        blocks = []
        for g in range(n_rows // 8):
            rolled = pltpu.roll(base, 8 * g, 1, stride=1, stride_axis=0)
            blocks.append(rolled[:, lo:hi])
        return jnp.concatenate(blocks, axis=0)

    row_c = lax.broadcasted_iota(jnp.int32, (BAND_Q, 3 * BAND_Q), 0) // CHUNK
    col = lax.broadcasted_iota(jnp.int32, (BAND_Q, 3 * BAND_Q), 1)
    col_c = (col % BAND_Q) // CHUNK
    visible = ((col >= BAND_Q) | (row_c <= col_c)) & ((col < 2 * BAND_Q) | (col_c <= row_c))
    shift = shift_ref[pl.program_id(0)]
    bp_ref[0] = jnp.where(visible, toeplitz(xp_ref[0], BAND_Q, BAND_Q, ROLL_WIDTH) * LOG2E - shift, NEG)
    bs_ref[0] = toeplitz(xs_ref[0], bs_ref.shape[1], LANES, LANES + bs_ref.shape[2])


def _rel_bias_blocks(table, n_new, shift):
    assert 2 * BAND_Q == WINDOW and 3 * BAND_Q + BAND_Q == ROLL_WIDTH
    rev = table[:, ::-1]
    n_rel = table.shape[1]
    lead_p = 3 * BAND_Q - MAX_REL
    lead_s = WINDOW + LANES - MAX_REL
    row_p = jnp.pad(rev, ((0, 0), (lead_p, ROLL_WIDTH - n_rel - lead_p)), mode="edge")
    row_s = jnp.pad(rev, ((0, 0), (lead_s, ROLL_WIDTH - n_rel - lead_s)), mode="edge")
    h = table.shape[0]
    ws = WINDOW + LANES
    return pl.pallas_call(
        _bias_kernel,
        grid=(h,),
        in_specs=[pl.BlockSpec(memory_space=pltpu.SMEM),
                  pl.BlockSpec((1, 1, ROLL_WIDTH), lambda i: (i, 0, 0)),
                  pl.BlockSpec((1, 1, ROLL_WIDTH), lambda i: (i, 0, 0))],
        out_specs=[pl.BlockSpec((1, BAND_Q, 3 * BAND_Q), lambda i: (i, 0, 0)),
                   pl.BlockSpec((1, n_new, ws), lambda i: (i, 0, 0))],
        out_shape=[jax.ShapeDtypeStruct((h, BAND_Q, 3 * BAND_Q), F32),
                   jax.ShapeDtypeStruct((h, n_new, ws), F32)],
        compiler_params=_params("arbitrary"),
        name="rel_bias",
    )(shift, row_p.reshape(h, 1, ROLL_WIDTH), row_s.reshape(h, 1, ROLL_WIDTH))


def _band_kernel(shifted_ref, q_ref, k_ref, v_ref, b_ref, o_ref, *, batch):
    tq = BAND_Q
    n_sub = q_ref.shape[1] // tq
    step = pl.program_id(1)

    def attend(shifted):
        for sub in range(n_sub):
            i = step * n_sub + sub
            q = q_ref[0, sub * tq:(sub + 1) * tq, :]
            s_parts, v_parts = [], []
            for t in range(3):
                start = pl.multiple_of(jnp.maximum(i - 2 + t, 0) * tq, tq)
                kb = k_ref[0, pl.ds(start, tq), :]
                v_parts.append(v_ref[0, pl.ds(start, tq), :])
                s = lax.dot_general(q, kb, _NT, preferred_element_type=F32)
                s = s + b_ref[0, :, t * tq:(t + 1) * tq]
                if sub + t < 2:
                    s = s + jnp.where(i + t >= 2, 0.0, NEG)
                s_parts.append(s)
            s = jnp.concatenate(s_parts, axis=1)
            if not shifted:
                s = s - jnp.max(s, axis=-1, keepdims=True)
            p = jnp.exp2(s)
            l = jnp.sum(p, axis=-1, keepdims=True)
            pb = p.astype(BF16)
            acc = jnp.dot(pb[:, :tq], v_parts[0], preferred_element_type=F32)
            acc += jnp.dot(pb[:, tq:2 * tq], v_parts[1], preferred_element_type=F32)
            acc += jnp.dot(pb[:, 2 * tq:], v_parts[2], preferred_element_type=F32)
            o_ref[0, sub * tq:(sub + 1) * tq, :] = (acc / l).astype(o_ref.dtype)

    shifted = shifted_ref[pl.program_id(0) // batch] != 0
    pl.when(shifted)(lambda: attend(True))
    pl.when(jnp.logical_not(shifted))(lambda: attend(False))


def _band_attention(qb, kb, vb, bias, shifted, batch, seq):
    h, m, hd = qb.shape
    step = min(BAND_STEP, seq)
    nq = seq // step
    return pl.pallas_call(
        functools.partial(_band_kernel, batch=batch),
        grid=(h * batch, nq),
        in_specs=[pl.BlockSpec(memory_space=pltpu.SMEM),
                  pl.BlockSpec((1, step, hd), lambda g, i: (g // batch, (g % batch) * nq + i, 0)),
                  pl.BlockSpec((1, seq, hd), lambda g, i: (g // batch, g % batch, 0)),
                  pl.BlockSpec((1, seq, hd), lambda g, i: (g // batch, g % batch, 0)),
                  pl.BlockSpec((1, BAND_Q, 3 * BAND_Q), lambda g, i: (g // batch, 0, 0))],
        out_specs=pl.BlockSpec((1, step, hd), lambda g, i: (g // batch, (g % batch) * nq + i, 0)),
        out_shape=jax.ShapeDtypeStruct((h, m, hd), BF16),
        compiler_params=_params("arbitrary", "arbitrary"),
        name="band_attention",
    )(shifted, qb, kb, vb, bias)


def _band_sample_kernel(q_ref, kn_ref, vn_ref, kc_ref, vc_ref, b_ref, o_ref):
    n = q_ref.shape[0]
    lc = kc_ref.shape[0] // N_HEADS
    for h in range(N_HEADS):
        sl = slice(h * HEAD_DIM, (h + 1) * HEAD_DIM)
        rows = _head_rows(h, lc)
        q = q_ref[:, sl].astype(BF16)
        s_c = lax.dot_general(q, kc_ref[rows, :].astype(BF16), _NT, preferred_element_type=F32)
        s_n = lax.dot_general(q, kn_ref[:, sl].astype(BF16), _NT, preferred_element_type=F32)
        s_c = s_c + b_ref[h, :, :lc]
        s_n = s_n + b_ref[h, :, lc:lc + n]
        m = jnp.maximum(jnp.max(s_c, axis=-1, keepdims=True), jnp.max(s_n, axis=-1, keepdims=True))
        p_c = jnp.exp(s_c - m)
        p_n = jnp.exp(s_n - m)
        l = jnp.sum(p_c, axis=-1, keepdims=True) + jnp.sum(p_n, axis=-1, keepdims=True)
        acc = jnp.dot(p_c.astype(BF16), vc_ref[rows, :].astype(BF16), preferred_element_type=F32)
        acc += jnp.dot(p_n.astype(BF16), vn_ref[:, sl].astype(BF16), preferred_element_type=F32)
        o_ref[h] = (acc / l).astype(o_ref.dtype)


def _band_attention_sample(q, k_new, v_new, k_cache, v_cache, idx, bias, n):
    m, d = q.shape
    _, b, rows, hd = k_cache.shape
    tok = pl.BlockSpec((n, d), lambda i: (i, 0))
    cache = pl.BlockSpec((None, None, rows, hd), lambda i: (idx, i, 0, 0))
    return pl.pallas_call(
        _band_sample_kernel,
        grid=(b,),
        in_specs=[tok, tok, tok, cache, cache,
                  pl.BlockSpec(bias.shape, lambda i: (0, 0, 0))],
        out_specs=pl.BlockSpec((N_HEADS, n, HEAD_DIM), lambda i: (0, i, 0)),
        out_shape=jax.ShapeDtypeStruct((N_HEADS, m, HEAD_DIM), BF16),
        compiler_params=_params("arbitrary"),
        name="band_attention_sample",
    )(q, k_new, v_new, k_cache, v_cache, bias)


def _online_softmax_step(s, carry, v):
    m, l, acc = carry
    m_new = jnp.maximum(m, jnp.max(s, axis=-1, keepdims=True))
    alpha = jnp.exp2(m - m_new)
    p = jnp.exp2(s - m_new)
    l = alpha * l + jnp.sum(p, axis=-1, keepdims=True)
    acc = alpha * acc + jnp.dot(p.astype(BF16), v, preferred_element_type=F32)
    return m_new, l, acc


def _fox_kernel(reach_ref, q_ref, k_ref, v_ref, c_ref, *rest, n_stack):
    layer_k, layer_v = rest[:n_stack], rest[n_stack:2 * n_stack]
    o_ref = rest[2 * n_stack]
    if n_stack:
        k_stack, v_stack = rest[2 * n_stack + 1:2 * n_stack + 3]
        for t in range(n_stack):
            k_stack[t] = layer_k[t][0]
            v_stack[t] = layer_v[t][0]

    i = pl.program_id(1)
    tq, hd = q_ref.shape[1], q_ref.shape[2]
    blk = min(FOX_BLOCK, tq)
    tk = min(FOX_CHUNK, tq)
    n, per = tq // blk, tk // blk
    qs = [q_ref[0, a * blk:(a + 1) * blk, :] for a in range(n)]
    reach = reach_ref[0]
    base = pl.multiple_of(i * tq, tq)
    causal = (lax.broadcasted_iota(jnp.int32, (blk, blk), 1)
              <= lax.broadcasted_iota(jnp.int32, (blk, blk), 0))

    def c_rows(first_block, count):
        return [c_ref[0, pl.ds(first_block + t, 1), :] * LOG2E for t in range(count)]

    def c_min_blocks(first_block, count):
        mins = [jnp.min(c, axis=-1, keepdims=True) for c in c_rows(first_block, count)]
        return functools.reduce(jnp.minimum, mins)

    def scores(q, start, size):
        k = k_ref[0, pl.ds(start, size), :]
        return (lax.dot_general(q, k, _NT, preferred_element_type=F32),
                v_ref[0, pl.ds(start, size), :])

    def minus_rows(s, rows):
        return jnp.concatenate([s[:, t * blk:(t + 1) * blk] - r for t, r in enumerate(rows)], axis=1)

    def mask_own(s, a):
        own = jnp.where(causal, s[:, a * blk:], NEG)
        return own if a == 0 else jnp.concatenate([s[:, :a * blk], own], axis=1)

    def finish(a, l, acc):
        o_ref[0, a * blk:(a + 1) * blk, :] = (acc / l).astype(o_ref.dtype)

    def attend_exact():
        def full_chunk(j, carry):
            start = pl.multiple_of(j * tk, tk)
            out = []
            for a in range(n):
                s, v = scores(qs[a], start, tk)
                out.append(_online_softmax_step(minus_rows(s, c_rows(j * per, per)), carry[a], v))
            return tuple(out)

        init = (jnp.full((blk, 1), NEG, F32), jnp.zeros((blk, 1), F32), jnp.zeros((blk, hd), F32))
        carry = lax.fori_loop(0, i * (tq // tk), full_chunk, (init,) * n)
        for a in range(n):
            s, v = scores(qs[a], base, (a + 1) * blk)
            s = mask_own(minus_rows(s, c_rows(i * n, a + 1)), a)
            _, l, acc = _online_softmax_step(s, carry[a], v)
            finish(a, l, acc)

    def attend_bounded():
        def update(s, m_new, carry, v):
            m, l, acc = carry
            alpha = jnp.exp2(m - m_new)
            p = jnp.exp2(s)
            l = alpha * l + jnp.sum(p, axis=-1, keepdims=True)
            return m_new, l, alpha * acc + jnp.dot(p.astype(BF16), v, preferred_element_type=F32)

        def full_chunk(j, carry):
            start = pl.multiple_of(j * tk, tk)
            rows = c_rows(j * per, per)
            m_new = jnp.maximum(carry[0][0], reach - c_min_blocks(j * per, per))
            shifted = [r + m_new for r in rows]
            out = []
            for a in range(n):
                s, v = scores(qs[a], start, tk)
                out.append(update(minus_rows(s, shifted), m_new, carry[a], v))
            return tuple(out)

        init = (jnp.full((1, 1), NEG, F32), jnp.zeros((blk, 1), F32), jnp.zeros((blk, hd), F32))
        carry = lax.fori_loop(0, i * (tq // tk), full_chunk, (init,) * n)
        for a in range(n):
            rows = c_rows(i * n, a + 1)
            c_vis = jnp.min(jnp.where(causal, rows[-1], -NEG), axis=-1, keepdims=True)
            c_seen = c_vis if a == 0 else jnp.minimum(c_vis, c_min_blocks(i * n, a))
            m_new = jnp.maximum(carry[a][0], reach - c_seen)
            s, v = scores(qs[a], base, (a + 1) * blk)
            s = mask_own(minus_rows(s, rows) - m_new, a)
            _, l, acc = update(s, m_new, carry[a], v)
            finish(a, l, acc)

    use_bound = 2.0 * reach <= MAX_SLACK
    pl.when(use_bound)(attend_bounded)
    pl.when(jnp.logical_not(use_bound))(attend_exact)


def _fox_attention(qb, kb, vb, c_rows, reach, batch, seq, stack_kv=()):
    h, m, hd = qb.shape
    t = min(FOX_STEP, seq)
    nq = seq // t
    blk = min(FOX_BLOCK, t)
    c_blocks = c_rows.reshape(h, m // blk, blk)
    in_specs = [pl.BlockSpec(memory_space=pltpu.SMEM),
                pl.BlockSpec((1, t, hd), lambda g, i: (g // batch, (g % batch) * nq + i, 0)),
                pl.BlockSpec((1, seq, hd), lambda g, i: (g // batch, g % batch, 0)),
                pl.BlockSpec((1, seq, hd), lambda g, i: (g // batch, g % batch, 0)),
                pl.BlockSpec((1, seq // blk, blk), lambda g, i: (g // batch, g % batch, 0))]
    out_specs = [pl.BlockSpec((1, t, hd), lambda g, i: (g // batch, (g % batch) * nq + i, 0))]
    out_shape = [jax.ShapeDtypeStruct((h, m, hd), BF16)]
    n_stack = len(stack_kv)
    extra = []
    if n_stack:
        rows = stack_kv[0][0].shape[1]
        steps = h * batch * nq
        assert rows % (steps * 8) == 0
        per = rows // steps
        in_specs += [pl.BlockSpec((1, per, hd), lambda g, i: (0, g * nq + i, 0))] * (2 * n_stack)
        extra = [kv[0] for kv in stack_kv] + [kv[1] for kv in stack_kv]
        out_specs += [pl.BlockSpec((n_stack, per, hd), lambda g, i: (0, g * nq + i, 0))] * 2
        out_shape += [jax.ShapeDtypeStruct((n_stack, rows, hd), F32)] * 2
    return pl.pallas_call(
        functools.partial(_fox_kernel, n_stack=n_stack),
        grid=(h * batch, nq),
        in_specs=in_specs,
        out_specs=out_specs,
        out_shape=out_shape,
        compiler_params=_params("arbitrary", "arbitrary"),
        name="fox_attention",
    )(reach.reshape(1), qb, kb, vb, c_blocks, *extra)


def _fox_sample_kernel(q_ref, kn_ref, vn_ref, kc_ref, vc_ref, cc_ref, lfn_ref, u_ref, o_ref):
    n = q_ref.shape[0]
    c_cache = cc_ref[0]
    lc = c_cache.shape[1]
    s3 = jnp.dot(_split3(lfn_ref[0]), u_ref[...], preferred_element_type=F32)
    c_new = s3[:N_HEADS] + s3[N_HEADS:2 * N_HEADS] + s3[2 * N_HEADS:] + c_cache[:, lc - 1:lc]
    row = lax.broadcasted_iota(jnp.int32, (n, n), 0)
    col = lax.broadcasted_iota(jnp.int32, (n, n), 1)
    for h in range(N_HEADS):
        sl = slice(h * HEAD_DIM, (h + 1) * HEAD_DIM)
        rows = _head_rows(h, lc)
        q = q_ref[:, sl].astype(BF16)
        s_c = lax.dot_general(q, kc_ref[rows, :].astype(BF16), _NT, preferred_element_type=F32)
        s_n = lax.dot_general(q, kn_ref[:, sl].astype(BF16), _NT, preferred_element_type=F32)
        s_c = s_c - c_cache[h:h + 1, :]
        s_n = jnp.where(col <= row, s_n - c_new[h:h + 1, :], NEG)
        m = jnp.maximum(jnp.max(s_c, axis=-1, keepdims=True), jnp.max(s_n, axis=-1, keepdims=True))
        p_c = jnp.exp(s_c - m)
        p_n = jnp.exp(s_n - m)
        l = jnp.sum(p_c, axis=-1, keepdims=True) + jnp.sum(p_n, axis=-1, keepdims=True)
        acc = jnp.dot(p_c.astype(BF16), vc_ref[rows, :].astype(BF16), preferred_element_type=F32)
        acc += jnp.dot(p_n.astype(BF16), vn_ref[:, sl].astype(BF16), preferred_element_type=F32)
        o_ref[h] = (acc / l).astype(o_ref.dtype)


def _fox_attention_sample(q, k_new, v_new, k_cache, v_cache, idx, c_cache, lf_new, n):
    m, d = q.shape
    _, b, rows, hd = k_cache.shape
    lc = rows // N_HEADS
    tok = pl.BlockSpec((n, d), lambda i: (i, 0))
    cache = pl.BlockSpec((None, None, rows, hd), lambda i: (idx, i, 0, 0))
    upper = jnp.triu(jnp.ones((n, n), BF16))
    return pl.pallas_call(
        _fox_sample_kernel,
        grid=(b,),
        in_specs=[tok, tok, tok, cache, cache,
                  pl.BlockSpec((1, N_HEADS, lc), lambda i: (i, 0, 0)),
                  pl.BlockSpec((1, N_HEADS, n), lambda i: (i, 0, 0)),
                  pl.BlockSpec((n, n), lambda i: (0, 0))],
        out_specs=pl.BlockSpec((N_HEADS, n, HEAD_DIM), lambda i: (0, i, 0)),
        out_shape=jax.ShapeDtypeStruct((N_HEADS, m, HEAD_DIM), BF16),
        compiler_params=_params("arbitrary"),
        name="fox_attention_sample",
    )(q, k_new, v_new, k_cache, v_cache, c_cache, lf_new, upper)


def _out_proj_kernel(a_ref, w_ref, x_ref, o_ref):
    a = jnp.concatenate([a_ref[h] for h in range(N_HEADS)], axis=1)
    o_ref[...] = x_ref[...] + jnp.dot(a, w_ref[...], preferred_element_type=F32)


def _out_proj(attn, w_o, x):
    m, d = x.shape
    tm = min(ROW_TILE, m)
    return pl.pallas_call(
        _out_proj_kernel,
        grid=(m // tm,),
        in_specs=[pl.BlockSpec((N_HEADS, tm, HEAD_DIM), lambda i: (0, i, 0)),
                  pl.BlockSpec((d, d), lambda i: (0, 0)),
                  pl.BlockSpec((tm, d), lambda i: (i, 0))],
        out_specs=pl.BlockSpec((tm, d), lambda i: (i, 0)),
        out_shape=jax.ShapeDtypeStruct((m, d), F32),
        compiler_params=_params("arbitrary"),
        name="out_proj",
    )(attn, w_o, x)


def _ffn_kernel(x_ref, g_ref, wg_ref, wu_ref, wd_ref, o_ref, xn_ref):
    @pl.when(pl.program_id(1) == 0)
    def _():
        x = x_ref[...]
        ms = jnp.mean(x * x, axis=-1, keepdims=True)
        xn_ref[...] = (x * lax.rsqrt(ms + EPS) * g_ref[...]).astype(BF16)
        o_ref[...] = x

    xn = xn_ref[...]
    gate = jnp.dot(xn, wg_ref[...], preferred_element_type=F32)
    up = jnp.dot(xn, wu_ref[...], preferred_element_type=F32)
    hidden = (gate * jax.nn.sigmoid(gate) * up).astype(BF16)
    o_ref[...] += jnp.dot(hidden, wd_ref[...], preferred_element_type=F32)


def _ffn(x, g_ffn, w_gate, w_up, w_down):
    m, d = x.shape
    f = w_gate.shape[1]
    tm = min(FFN_ROW_TILE, m)
    tf = FFN_TILE
    return pl.pallas_call(
        _ffn_kernel,
        grid=(m // tm, f // tf),
        in_specs=[pl.BlockSpec((tm, d), lambda i, j: (i, 0)),
                  pl.BlockSpec((1, d), lambda i, j: (0, 0)),
                  pl.BlockSpec((d, tf), lambda i, j: (0, j)),
                  pl.BlockSpec((d, tf), lambda i, j: (0, j)),
                  pl.BlockSpec((tf, d), lambda i, j: (j, 0))],
        out_specs=pl.BlockSpec((tm, d), lambda i, j: (i, 0)),
        out_shape=jax.ShapeDtypeStruct((m, d), F32),
        scratch_shapes=[pltpu.VMEM((tm, d), BF16)],
        compiler_params=_params("arbitrary", "arbitrary", vmem=FFN_VMEM_LIMIT),
        name="ffn",
    )(x, g_ffn.reshape(1, d), w_gate, w_up, w_down)


def kernel(x_prompt, x_sample, cache_a_k, cache_a_v, cache_b_k, cache_b_v, cache_b_logf,
           g_attn, w_qkv, g_q, g_k, w_o, rel_table, w_f, b_f, g_ffn, w_gate, w_up, w_down):
    batch, seq, d = x_prompt.shape
    dec_batch, n_new, _ = x_sample.shape
    depth = g_attn.shape[0]
    n_a, n_b = cache_a_k.shape[0], cache_b_k.shape[0]
    keep = min(WINDOW, seq)
    h, hd = N_HEADS, HEAD_DIM
    la, past = cache_a_k.shape[2], cache_b_k.shape[2]
    assert d == h * hd and keep == WINDOW and la == WINDOW

    w_f_b = w_f.astype(BF16)
    w_qkv_l = w_qkv[0].astype(BF16)

    cache_a_k = cache_a_k.reshape(n_a, dec_batch, la * h, hd)
    cache_a_v = cache_a_v.reshape(n_a, dec_batch, la * h, hd)
    cache_b_k = cache_b_k.reshape(n_b, dec_batch, past * h, hd)
    cache_b_v = cache_b_v.reshape(n_b, dec_batch, past * h, hd)

    yp = x_prompt.reshape(batch * seq, d)
    ys = x_sample.reshape(dec_batch * n_new, d)
    a_kv = [jnp.zeros((n_a, batch * keep * h, hd), F32) for _ in range(2)]
    b_layers, b_kv = [], None
    b_fp, a_ks, a_vs, b_ks, b_vs, b_fs = [], [], [], [], [], []

    for layer in range(depth):
        idx = layer // 2
        common = (g_attn[layer], w_qkv_l, g_q[layer], g_k[layer])
        cast = [(w_o, layer), (w_gate, layer), (w_up, layer), (w_down, layer)]
        if layer + 1 < depth:
            cast.append((w_qkv, layer + 1))
        reach = _qk_reach(g_q[layer], g_k[layer])
        if layer % 2 == 0:
            (qb, kb, vb, *a_kv), cast_w = _qkv_proj(
                yp, *common, sample=False, q_scale=SCALE * LOG2E, kv_last=keep, seq=seq,
                kv_slots=n_a, kv_slot=idx, kv_prev=a_kv, cast=cast)
            (qs, ks, vs), _ = _qkv_proj(ys, *common, sample=True)
            b_hi = jnp.max(rel_table[idx], axis=1) * LOG2E
            b_lo = jnp.min(rel_table[idx], axis=1) * LOG2E
            shifted = 2.0 * reach + (b_hi - b_lo) <= MAX_SLACK
            bias_p, bias_s = _rel_bias_blocks(rel_table[idx], n_new,
                                              jnp.where(shifted, reach + b_hi, 0.0))
            mp = _band_attention(qb, kb, vb, bias_p, shifted.astype(jnp.int32), batch, seq)
            ms = _band_attention_sample(qs, ks, vs, cache_a_k, cache_a_v, idx, bias_s, n_new)
            a_ks.append(ks)
            a_vs.append(vs)
        else:
            fw = (w_f_b[idx], b_f[idx])
            (qb, kb, vb, k_all, v_all, lf), cast_w = _qkv_proj(
                yp, *common, fw, sample=False, q_scale=SCALE * LOG2E, kv_slots=1, cast=cast)
            b_layers.append((k_all, v_all))
            (qs, ks, vs, lfs), _ = _qkv_proj(ys, *common, fw, sample=True)
            last_b = idx == n_b - 1
            mp, *b_kv = _fox_attention(qb, kb, vb, _cumsum_lanes(lf.T, seq), reach, batch, seq,
                                       stack_kv=b_layers if last_b else ())
            lf_cache = cache_b_logf[idx].transpose(0, 2, 1).reshape(dec_batch * h, past)
            c_cache = _cumsum_lanes(lf_cache, past).reshape(dec_batch, h, past)
            lf_new = lfs.reshape(dec_batch, n_new, h).transpose(0, 2, 1)
            ms = _fox_attention_sample(qs, ks, vs, cache_b_k, cache_b_v, idx, c_cache, lf_new, n_new)
            b_fp.append(lf.reshape(batch, seq, h))
            b_ks.append(ks)
            b_vs.append(vs)
            b_fs.append(lfs.reshape(dec_batch, n_new, h))
        w_o_l, w_gate_l, w_up_l, w_down_l = cast_w[:4]
        if layer + 1 < depth:
            w_qkv_l = cast_w[4]
        yp = _out_proj(mp, w_o_l, yp)
        ys = _out_proj(ms, w_o_l, ys)
        yp = _ffn(yp, g_ffn[layer], w_gate_l, w_up_l, w_down_l)
        ys = _ffn(ys, g_ffn[layer], w_gate_l, w_up_l, w_down_l)

    def new_kv(parts):
        return jnp.stack(parts).reshape(len(parts), dec_batch, n_new, h, hd)

    return (yp.reshape(batch, seq, d), ys.reshape(dec_batch, n_new, d),
            a_kv[0].reshape(n_a, batch, keep, h, hd), a_kv[1].reshape(n_a, batch, keep, h, hd),
            b_kv[0].reshape(n_b, batch, seq, h, hd), b_kv[1].reshape(n_b, batch, seq, h, hd),
            jnp.stack(b_fp), new_kv(a_ks), new_kv(a_vs), new_kv(b_ks), new_kv(b_vs), jnp.stack(b_fs))
```

```python
import functools
import math

import jax
import jax.numpy as jnp
from jax import lax
from jax.experimental import pallas as pl
from jax.experimental.pallas import tpu as pltpu

N_HEADS = 16
HEAD_DIM = 128
CHUNK = 64
N_LEFT_CHUNKS = 8
WINDOW = N_LEFT_CHUNKS * CHUNK
MAX_REL = 128
EPS = 1e-6
SCALE = HEAD_DIM ** -0.5
LOG2E = math.log2(math.e)
NEG = -1e30

BF16 = jnp.bfloat16
F32 = jnp.float32

LANES = 128
HEADS_PER_TILE = 4
ROW_TILE = 512
FFN_TILE = 512
FFN_ROW_TILE = 1024
BIG_VMEM_LIMIT = 62 * 1024 * 1024
BAND_Q = 256
BAND_STEP = 2048
FOX_BLOCK = 512
FOX_CHUNK = 2048
FOX_STEP = 2048
MAX_SLACK = 96.0
BOUND_MARGIN = 1.02
SCAN_BLOCK = 512
ROLL_WIDTH = 1024
VMEM_LIMIT = 56 * 1024 * 1024

_NT = (((1,), (1,)), ((), ()))


def _params(*sem, vmem=VMEM_LIMIT):
    return pltpu.CompilerParams(dimension_semantics=sem, vmem_limit_bytes=vmem)


def _log_sigmoid(z):
    return jnp.minimum(z, 0.0) - jnp.log1p(jnp.exp(-jnp.abs(z)))


def _qk_reach(g_q, g_k):
    return HEAD_DIM * jnp.max(jnp.abs(g_q)) * jnp.max(jnp.abs(g_k)) * (SCALE * LOG2E * BOUND_MARGIN)


def _split3(x):
    hi = x.astype(BF16).astype(F32)
    r1 = x - hi
    mid = r1.astype(BF16).astype(F32)
    lo = r1 - mid
    return jnp.concatenate([hi, mid, lo], axis=0).astype(BF16)


def _head_rows(head, n_rows):
    return pl.ds(head, n_rows, stride=N_HEADS)


def _cast_plan(cast, steps, step_index):
    in_specs, args, out_specs, out_shape = [], [], [], []
    for stack, layer in cast:
        _, rows, cols = stack.shape
        group = 1
        while rows * group % (steps * 16):
            group *= 2
            assert steps % group == 0
        rb = rows * group // steps
        in_specs.append(pl.BlockSpec(
            (None, rb, cols),
            lambda *ids, layer=layer, group=group: (layer, step_index(*ids) // group, 0)))
        args.append(stack)
        out_specs.append(pl.BlockSpec(
            (rb, cols), lambda *ids, group=group: (step_index(*ids) // group, 0)))
        out_shape.append(jax.ShapeDtypeStruct((rows, cols), BF16))
    return in_specs, args, out_specs, out_shape


def _cast_blocks(srcs, dsts):
    for src, dst in zip(srcs, dsts):
        dst[...] = src[...].astype(BF16)


def _qkv_kernel(*refs, forget, sample, q_scale, n_alias):
    x_ref, ga_ref, wq_ref, wk_ref, wv_ref, gq_ref, gk_ref = refs[:7]
    rest = list(refs[7:])
    if forget:
        wf_ref, bf_ref = rest[:2]
        rest = rest[2:]
    rest = rest[n_alias:]
    if sample:
        q_out, k_out, v_out = rest[:3]
        rest = rest[3:]
    else:
        qb_out, kb_out, vb_out, k_out, v_out = rest[:5]
        rest = rest[5:]
    if forget:
        lf_out = rest[0]
        rest = rest[1:]
    (xn_ref,) = rest
    j = pl.program_id(1)
    tm = x_ref.shape[0]

    @pl.when(j == 0)
    def _():
        x = x_ref[...]
        ms = jnp.mean(x * x, axis=-1, keepdims=True)
        xn = (x * lax.rsqrt(ms + EPS) * ga_ref[...]).astype(BF16)
        xn_ref[...] = xn
        if forget:
            z = jnp.dot(xn, wf_ref[...], preferred_element_type=F32) + bf_ref[...]
            lf_out[...] = _log_sigmoid(z)

    xn = xn_ref[...]
    q = jnp.dot(xn, wq_ref[...], preferred_element_type=F32)
    k = jnp.dot(xn, wk_ref[...], preferred_element_type=F32)
    v = jnp.dot(xn, wv_ref[...], preferred_element_type=F32)
    gq = gq_ref[...]
    gk = gk_ref[...]
    for hh in range(HEADS_PER_TILE):
        sl = slice(hh * HEAD_DIM, (hh + 1) * HEAD_DIM)
        qh = q[:, sl]
        kh = k[:, sl]
        vh = v[:, sl]
        qn = qh * lax.rsqrt(jnp.mean(qh * qh, axis=-1, keepdims=True) + EPS) * gq * q_scale
        kn = kh * lax.rsqrt(jnp.mean(kh * kh, axis=-1, keepdims=True) + EPS) * gk
        if sample:
            q_out[:, sl] = qn
            k_out[:, sl] = kn
            v_out[:, sl] = vh
        else:
            qb_out[hh] = qn.astype(BF16)
            kb_out[hh] = kn.astype(BF16)
            vb_out[hh] = vh.astype(BF16)
            rows = _head_rows(j * HEADS_PER_TILE + hh, tm)
            k_out[rows, :] = kn
            v_out[rows, :] = vh


def _qkv_proj(x, g_attn, w_qkv, g_q, g_k, forget_w=None, *, sample, q_scale=SCALE,
              kv_last=None, seq=None, kv_slots=None, kv_slot=0, kv_prev=None):
    m, d = x.shape
    tm = min(ROW_TILE, m)
    tn = HEADS_PER_TILE * HEAD_DIM
    nj = d // tn
    grid = (m // tm, nj)
    forget = forget_w is not None

    in_specs = [
        pl.BlockSpec((tm, d), lambda i, j: (i, 0)),
        pl.BlockSpec((1, d), lambda i, j: (0, 0)),
        pl.BlockSpec((d, tn), lambda i, j: (0, j)),
        pl.BlockSpec((d, tn), lambda i, j: (0, nj + j)),
        pl.BlockSpec((d, tn), lambda i, j: (0, 2 * nj + j)),
        pl.BlockSpec((1, HEAD_DIM), lambda i, j: (0, 0)),
        pl.BlockSpec((1, HEAD_DIM), lambda i, j: (0, 0)),
    ]
    args = [x, g_attn.reshape(1, d), w_qkv, w_qkv, w_qkv,
            g_q.reshape(1, HEAD_DIM), g_k.reshape(1, HEAD_DIM)]
    if forget:
        w_f, b_f = forget_w
        in_specs += [pl.BlockSpec((d, N_HEADS), lambda i, j: (0, 0)),
                     pl.BlockSpec((1, N_HEADS), lambda i, j: (0, 0))]
        args += [w_f, b_f.reshape(1, N_HEADS)]

    aliases = {}
    if sample:
        tok_spec = pl.BlockSpec((tm, tn), lambda i, j: (i, j))
        out_shape = [jax.ShapeDtypeStruct((m, d), F32)] * 3
        out_specs = [tok_spec] * 3
    else:
        hm_shape = jax.ShapeDtypeStruct((N_HEADS, m, HEAD_DIM), BF16)
        hm_spec = pl.BlockSpec((HEADS_PER_TILE, tm, HEAD_DIM), lambda i, j: (j, i, 0))
        if kv_last is None:
            kv_rows = m
            kv_spec = pl.BlockSpec((None, tm * N_HEADS, HEAD_DIM), lambda i, j: (kv_slot, i, 0))
        else:
            assert kv_last == tm and seq % tm == 0
            per_seq = seq // tm
            kv_rows = m // seq * kv_last
            kv_spec = pl.BlockSpec((None, tm * N_HEADS, HEAD_DIM),
                                   lambda i, j: (kv_slot, i // per_seq, 0))
        kv_shape = jax.ShapeDtypeStruct((kv_slots, kv_rows * N_HEADS, HEAD_DIM), F32)
        out_shape = [hm_shape] * 3 + [kv_shape] * 2
        out_specs = [hm_spec] * 3 + [kv_spec] * 2
        if kv_prev is not None:
            assert kv_prev[0].shape == kv_shape.shape
            aliases = {len(args): 3, len(args) + 1: 4}
            in_specs += [pl.BlockSpec(memory_space=pl.ANY)] * 2
            args += list(kv_prev)
    if forget:
        out_shape += [jax.ShapeDtypeStruct((m, N_HEADS), F32)]
        out_specs += [pl.BlockSpec((tm, N_HEADS), lambda i, j: (i, 0))]

    return pl.pallas_call(
        functools.partial(_qkv_kernel, forget=forget, sample=sample, q_scale=q_scale,
                          n_alias=len(aliases)),
        grid=grid,
        in_specs=in_specs,
        out_specs=out_specs,
        out_shape=out_shape,
        scratch_shapes=[pltpu.VMEM((tm, d), BF16)],
        input_output_aliases=aliases,
        compiler_params=_params("arbitrary", "arbitrary"),
        name="qkv_sample" if sample else "qkv_prompt",
    )(*args)


def _cumsum_kernel(x_ref, u_ref, o_ref, carry_ref, *, blocks_per_seq):
    @pl.when(pl.program_id(0) % blocks_per_seq == 0)
    def _():
        carry_ref[...] = jnp.zeros_like(carry_ref)

    x = x_ref[...]
    r, t = x.shape
    s = jnp.dot(_split3(x), u_ref[...], preferred_element_type=F32)
    c = s[:r] + s[r:2 * r] + s[2 * r:] + carry_ref[:, :1]
    o_ref[...] = c
    carry_ref[...] = jnp.broadcast_to(c[:, t - 1:t], carry_ref.shape)


def _cumsum_lanes(x, seq):
    r, l = x.shape
    t = min(SCAN_BLOCK, seq)
    upper = jnp.triu(jnp.ones((t, t), BF16))
    return pl.pallas_call(
        functools.partial(_cumsum_kernel, blocks_per_seq=seq // t),
        grid=(l // t,),
        in_specs=[pl.BlockSpec((r, t), lambda i: (0, i)),
                  pl.BlockSpec((t, t), lambda i: (0, 0))],
        out_specs=pl.BlockSpec((r, t), lambda i: (0, i)),
        out_shape=jax.ShapeDtypeStruct((r, l), F32),
        scratch_shapes=[pltpu.VMEM((r, LANES), F32)],
        compiler_params=_params("arbitrary"),
        name="cumsum_lanes",
    )(x, upper)


def _bias_kernel(shift_ref, xp_ref, xs_ref, bp_ref, bs_ref):
    def toeplitz(row, n_rows, lo, hi):
        base = jnp.broadcast_to(row, (8, ROLL_WIDTH))
        blocks = []
        for g in range(n_rows // 8):
            rolled = pltpu.roll(base, 8 * g, 1, stride=1, stride_axis=0)
            blocks.append(rolled[:, lo:hi])
        return jnp.concatenate(blocks, axis=0)

    row_c = lax.broadcasted_iota(jnp.int32, (BAND_Q, 3 * BAND_Q), 0) // CHUNK
    col = lax.broadcasted_iota(jnp.int32, (BAND_Q, 3 * BAND_Q), 1)
    col_c = (col % BAND_Q) // CHUNK
    visible = ((col >= BAND_Q) | (row_c <= col_c)) & ((col < 2 * BAND_Q) | (col_c <= row_c))
    shift = shift_ref[pl.program_id(0)]
    bp_ref[0] = jnp.where(visible, toeplitz(xp_ref[0], BAND_Q, BAND_Q, ROLL_WIDTH) * LOG2E - shift, NEG)
    bs_ref[0] = toeplitz(xs_ref[0], bs_ref.shape[1], LANES, LANES + bs_ref.shape[2])


def _rel_bias_blocks(table, n_new, shift):
    assert 2 * BAND_Q == WINDOW and 3 * BAND_Q + BAND_Q == ROLL_WIDTH
    rev = table[:, ::-1]
    n_rel = table.shape[1]
    lead_p = 3 * BAND_Q - MAX_REL
    lead_s = WINDOW + LANES - MAX_REL
    row_p = jnp.pad(rev, ((0, 0), (lead_p, ROLL_WIDTH - n_rel - lead_p)), mode="edge")
    row_s = jnp.pad(rev, ((0, 0), (lead_s, ROLL_WIDTH - n_rel - lead_s)), mode="edge")
    h = table.shape[0]
    ws = WINDOW + LANES
    return pl.pallas_call(
        _bias_kernel,
        grid=(h,),
        in_specs=[pl.BlockSpec(memory_space=pltpu.SMEM),
                  pl.BlockSpec((1, 1, ROLL_WIDTH), lambda i: (i, 0, 0)),
                  pl.BlockSpec((1, 1, ROLL_WIDTH), lambda i: (i, 0, 0))],
        out_specs=[pl.BlockSpec((1, BAND_Q, 3 * BAND_Q), lambda i: (i, 0, 0)),
                   pl.BlockSpec((1, n_new, ws), lambda i: (i, 0, 0))],
        out_shape=[jax.ShapeDtypeStruct((h, BAND_Q, 3 * BAND_Q), F32),
                   jax.ShapeDtypeStruct((h, n_new, ws), F32)],
        compiler_params=_params("arbitrary"),
        name="rel_bias",
    )(shift, row_p.reshape(h, 1, ROLL_WIDTH), row_s.reshape(h, 1, ROLL_WIDTH))


def _band_kernel(shifted_ref, q_ref, k_ref, v_ref, b_ref, *rest, batch, n_cast):
    o_ref = rest[n_cast]
    _cast_blocks(rest[:n_cast], rest[n_cast + 1:])
    tq = BAND_Q
    n_sub = q_ref.shape[1] // tq
    step = pl.program_id(1)

    def attend(shifted):
        for sub in range(n_sub):
            i = step * n_sub + sub
            q = q_ref[0, sub * tq:(sub + 1) * tq, :]
            s_parts, v_parts = [], []
            for t in range(3):
                start = pl.multiple_of(jnp.maximum(i - 2 + t, 0) * tq, tq)
                kb = k_ref[0, pl.ds(start, tq), :]
                v_parts.append(v_ref[0, pl.ds(start, tq), :])
                s = lax.dot_general(q, kb, _NT, preferred_element_type=F32)
                s = s + b_ref[0, :, t * tq:(t + 1) * tq]
                if sub + t < 2:
                    s = s + jnp.where(i + t >= 2, 0.0, NEG)
                s_parts.append(s)
            s = jnp.concatenate(s_parts, axis=1)
            if not shifted:
                s = s - jnp.max(s, axis=-1, keepdims=True)
            p = jnp.exp2(s)
            l = jnp.sum(p, axis=-1, keepdims=True)
            pb = p.astype(BF16)
            acc = jnp.dot(pb[:, :tq], v_parts[0], preferred_element_type=F32)
            acc += jnp.dot(pb[:, tq:2 * tq], v_parts[1], preferred_element_type=F32)
            acc += jnp.dot(pb[:, 2 * tq:], v_parts[2], preferred_element_type=F32)
            o_ref[0, sub * tq:(sub + 1) * tq, :] = (acc / l).astype(o_ref.dtype)

    shifted = shifted_ref[pl.program_id(0) // batch] != 0
    pl.when(shifted)(lambda: attend(True))
    pl.when(jnp.logical_not(shifted))(lambda: attend(False))


def _band_attention(qb, kb, vb, bias, shifted, batch, seq, cast=()):
    h, m, hd = qb.shape
    step = min(BAND_STEP, seq)
    nq = seq // step
    c_in, c_args, c_out, c_shape = _cast_plan(cast, h * batch * nq, lambda g, i: g * nq + i)
    return pl.pallas_call(
        functools.partial(_band_kernel, batch=batch, n_cast=len(cast)),
        grid=(h * batch, nq),
        in_specs=[pl.BlockSpec(memory_space=pltpu.SMEM),
                  pl.BlockSpec((1, step, hd), lambda g, i: (g // batch, (g % batch) * nq + i, 0)),
                  pl.BlockSpec((1, seq, hd), lambda g, i: (g // batch, g % batch, 0)),
                  pl.BlockSpec((1, seq, hd), lambda g, i: (g // batch, g % batch, 0)),
                  pl.BlockSpec((1, BAND_Q, 3 * BAND_Q), lambda g, i: (g // batch, 0, 0))] + c_in,
        out_specs=[pl.BlockSpec((1, step, hd),
                                lambda g, i: (g // batch, (g % batch) * nq + i, 0))] + c_out,
        out_shape=[jax.ShapeDtypeStruct((h, m, hd), BF16)] + c_shape,
        compiler_params=_params("arbitrary", "arbitrary"),
        name="band_attention",
    )(shifted, qb, kb, vb, bias, *c_args)


def _band_sample_kernel(q_ref, kn_ref, vn_ref, kc_ref, vc_ref, b_ref, o_ref):
    n = q_ref.shape[0]
    lc = kc_ref.shape[0] // N_HEADS
    for h in range(N_HEADS):
        sl = slice(h * HEAD_DIM, (h + 1) * HEAD_DIM)
        rows = _head_rows(h, lc)
        q = q_ref[:, sl].astype(BF16)
        s_c = lax.dot_general(q, kc_ref[rows, :].astype(BF16), _NT, preferred_element_type=F32)
        s_n = lax.dot_general(q, kn_ref[:, sl].astype(BF16), _NT, preferred_element_type=F32)
        s_c = s_c + b_ref[h, :, :lc]
        s_n = s_n + b_ref[h, :, lc:lc + n]
        m = jnp.maximum(jnp.max(s_c, axis=-1, keepdims=True), jnp.max(s_n, axis=-1, keepdims=True))
        p_c = jnp.exp(s_c - m)
        p_n = jnp.exp(s_n - m)
        l = jnp.sum(p_c, axis=-1, keepdims=True) + jnp.sum(p_n, axis=-1, keepdims=True)
        acc = jnp.dot(p_c.astype(BF16), vc_ref[rows, :].astype(BF16), preferred_element_type=F32)
        acc += jnp.dot(p_n.astype(BF16), vn_ref[:, sl].astype(BF16), preferred_element_type=F32)
        o_ref[h] = (acc / l).astype(o_ref.dtype)


def _band_attention_sample(q, k_new, v_new, k_cache, v_cache, idx, bias, n):
    m, d = q.shape
    _, b, rows, hd = k_cache.shape
    tok = pl.BlockSpec((n, d), lambda i: (i, 0))
    cache = pl.BlockSpec((None, None, rows, hd), lambda i: (idx, i, 0, 0))
    return pl.pallas_call(
        _band_sample_kernel,
        grid=(b,),
        in_specs=[tok, tok, tok, cache, cache,
                  pl.BlockSpec(bias.shape, lambda i: (0, 0, 0))],
        out_specs=pl.BlockSpec((N_HEADS, n, HEAD_DIM), lambda i: (0, i, 0)),
        out_shape=jax.ShapeDtypeStruct((N_HEADS, m, HEAD_DIM), BF16),
        compiler_params=_params("arbitrary"),
        name="band_attention_sample",
    )(q, k_new, v_new, k_cache, v_cache, bias)


def _online_softmax_step(s, carry, v):
    m, l, acc = carry
    m_new = jnp.maximum(m, jnp.max(s, axis=-1, keepdims=True))
    alpha = jnp.exp2(m - m_new)
    p = jnp.exp2(s - m_new)
    l = alpha * l + jnp.sum(p, axis=-1, keepdims=True)
    acc = alpha * acc + jnp.dot(p.astype(BF16), v, preferred_element_type=F32)
    return m_new, l, acc


def _fox_kernel(reach_ref, q_ref, k_ref, v_ref, c_ref, *rest, n_stack, n_cast):
    layer_k, layer_v = rest[:n_stack], rest[n_stack:2 * n_stack]
    rest = rest[2 * n_stack:]
    o_ref = rest[n_cast]
    _cast_blocks(rest[:n_cast], rest[len(rest) - n_cast:])
    if n_stack:
        k_stack, v_stack = rest[n_cast + 1:n_cast + 3]
        for t in range(n_stack):
            k_stack[t] = layer_k[t][0]
            v_stack[t] = layer_v[t][0]

    i = pl.program_id(1)
    tq, hd = q_ref.shape[1], q_ref.shape[2]
    blk = min(FOX_BLOCK, tq)
    tk = min(FOX_CHUNK, tq)
    n, per = tq // blk, tk // blk
    qs = [q_ref[0, a * blk:(a + 1) * blk, :] for a in range(n)]
    reach = reach_ref[0]
    base = pl.multiple_of(i * tq, tq)
    causal = (lax.broadcasted_iota(jnp.int32, (blk, blk), 1)
              <= lax.broadcasted_iota(jnp.int32, (blk, blk), 0))

    def c_rows(first_block, count):
        return [c_ref[0, pl.ds(first_block + t, 1), :] * LOG2E for t in range(count)]

    def c_min_blocks(first_block, count):
        mins = [jnp.min(c, axis=-1, keepdims=True) for c in c_rows(first_block, count)]
        return functools.reduce(jnp.minimum, mins)

    def scores(q, start, size):
        k = k_ref[0, pl.ds(start, size), :]
        return (lax.dot_general(q, k, _NT, preferred_element_type=F32),
                v_ref[0, pl.ds(start, size), :])

    def minus_rows(s, rows):
        return jnp.concatenate([s[:, t * blk:(t + 1) * blk] - r for t, r in enumerate(rows)], axis=1)

    def mask_own(s, a):
        own = jnp.where(causal, s[:, a * blk:], NEG)
        return own if a == 0 else jnp.concatenate([s[:, :a * blk], own], axis=1)

    def finish(a, l, acc):
        o_ref[0, a * blk:(a + 1) * blk, :] = (acc / l).astype(o_ref.dtype)

    def attend_exact():
        def full_chunk(j, carry):
            start = pl.multiple_of(j * tk, tk)
            out = []
            for a in range(n):
                s, v = scores(qs[a], start, tk)
                out.append(_online_softmax_step(minus_rows(s, c_rows(j * per, per)), carry[a], v))
            return tuple(out)

        init = (jnp.full((blk, 1), NEG, F32), jnp.zeros((blk, 1), F32), jnp.zeros((blk, hd), F32))
        carry = lax.fori_loop(0, i * (tq // tk), full_chunk, (init,) * n)
        for a in range(n):
            s, v = scores(qs[a], base, (a + 1) * blk)
            s = mask_own(minus_rows(s, c_rows(i * n, a + 1)), a)
            _, l, acc = _online_softmax_step(s, carry[a], v)
            finish(a, l, acc)

    def attend_bounded():
        def update(s, m_new, carry, v):
            m, l, acc = carry
            alpha = jnp.exp2(m - m_new)
            p = jnp.exp2(s)
            l = alpha * l + jnp.sum(p, axis=-1, keepdims=True)
            return m_new, l, alpha * acc + jnp.dot(p.astype(BF16), v, preferred_element_type=F32)

        def full_chunk(j, carry):
            start = pl.multiple_of(j * tk, tk)
            rows = c_rows(j * per, per)
            m_new = jnp.maximum(carry[0][0], reach - c_min_blocks(j * per, per))
            shifted = [r + m_new for r in rows]
            out = []
            for a in range(n):
                s, v = scores(qs[a], start, tk)
                out.append(update(minus_rows(s, shifted), m_new, carry[a], v))
            return tuple(out)

        init = (jnp.full((1, 1), NEG, F32), jnp.zeros((blk, 1), F32), jnp.zeros((blk, hd), F32))
        carry = lax.fori_loop(0, i * (tq // tk), full_chunk, (init,) * n)
        for a in range(n):
            rows = c_rows(i * n, a + 1)
            c_vis = jnp.min(jnp.where(causal, rows[-1], -NEG), axis=-1, keepdims=True)
            c_seen = c_vis if a == 0 else jnp.minimum(c_vis, c_min_blocks(i * n, a))
            m_new = jnp.maximum(carry[a][0], reach - c_seen)
            s, v = scores(qs[a], base, (a + 1) * blk)
            s = mask_own(minus_rows(s, rows) - m_new, a)
            _, l, acc = update(s, m_new, carry[a], v)
            finish(a, l, acc)

    use_bound = 2.0 * reach <= MAX_SLACK
    pl.when(use_bound)(attend_bounded)
    pl.when(jnp.logical_not(use_bound))(attend_exact)


def _fox_attention(qb, kb, vb, c_rows, reach, batch, seq, stack_kv=(), cast=()):
    h, m, hd = qb.shape
    t = min(FOX_STEP, seq)
    nq = seq // t
    blk = min(FOX_BLOCK, t)
    c_blocks = c_rows.reshape(h, m // blk, blk)
    in_specs = [pl.BlockSpec(memory_space=pltpu.SMEM),
                pl.BlockSpec((1, t, hd), lambda g, i: (g // batch, (g % batch) * nq + i, 0)),
                pl.BlockSpec((1, seq, hd), lambda g, i: (g // batch, g % batch, 0)),
                pl.BlockSpec((1, seq, hd), lambda g, i: (g // batch, g % batch, 0)),
                pl.BlockSpec((1, seq // blk, blk), lambda g, i: (g // batch, g % batch, 0))]
    out_specs = [pl.BlockSpec((1, t, hd), lambda g, i: (g // batch, (g % batch) * nq + i, 0))]
    out_shape = [jax.ShapeDtypeStruct((h, m, hd), BF16)]
    n_stack = len(stack_kv)
    extra = []
    if n_stack:
        rows = stack_kv[0][0].shape[1]
        steps = h * batch * nq
        assert rows % (steps * 8) == 0
        per = rows // steps
        in_specs += [pl.BlockSpec((1, per, hd), lambda g, i: (0, g * nq + i, 0))] * (2 * n_stack)
        extra = [kv[0] for kv in stack_kv] + [kv[1] for kv in stack_kv]
        out_specs += [pl.BlockSpec((n_stack, per, hd), lambda g, i: (0, g * nq + i, 0))] * 2
        out_shape += [jax.ShapeDtypeStruct((n_stack, rows, hd), F32)] * 2
    c_in, c_args, c_out, c_shape = _cast_plan(cast, h * batch * nq, lambda g, i: g * nq + i)
    return pl.pallas_call(
        functools.partial(_fox_kernel, n_stack=n_stack, n_cast=len(cast)),
        grid=(h * batch, nq),
        in_specs=in_specs + c_in,
        out_specs=out_specs + c_out,
        out_shape=out_shape + c_shape,
        compiler_params=_params("arbitrary", "arbitrary", vmem=BIG_VMEM_LIMIT),
        name="fox_attention",
    )(reach.reshape(1), qb, kb, vb, c_blocks, *extra, *c_args)


def _fox_sample_kernel(q_ref, kn_ref, vn_ref, kc_ref, vc_ref, cc_ref, lfn_ref, u_ref, o_ref):
    n = q_ref.shape[0]
    c_cache = cc_ref[0]
    lc = c_cache.shape[1]
    s3 = jnp.dot(_split3(lfn_ref[0]), u_ref[...], preferred_element_type=F32)
    c_new = s3[:N_HEADS] + s3[N_HEADS:2 * N_HEADS] + s3[2 * N_HEADS:] + c_cache[:, lc - 1:lc]
    row = lax.broadcasted_iota(jnp.int32, (n, n), 0)
    col = lax.broadcasted_iota(jnp.int32, (n, n), 1)
    for h in range(N_HEADS):
        sl = slice(h * HEAD_DIM, (h + 1) * HEAD_DIM)
        rows = _head_rows(h, lc)
        q = q_ref[:, sl].astype(BF16)
        s_c = lax.dot_general(q, kc_ref[rows, :].astype(BF16), _NT, preferred_element_type=F32)
        s_n = lax.dot_general(q, kn_ref[:, sl].astype(BF16), _NT, preferred_element_type=F32)
        s_c = s_c - c_cache[h:h + 1, :]
        s_n = jnp.where(col <= row, s_n - c_new[h:h + 1, :], NEG)
        m = jnp.maximum(jnp.max(s_c, axis=-1, keepdims=True), jnp.max(s_n, axis=-1, keepdims=True))
        p_c = jnp.exp(s_c - m)
        p_n = jnp.exp(s_n - m)
        l = jnp.sum(p_c, axis=-1, keepdims=True) + jnp.sum(p_n, axis=-1, keepdims=True)
        acc = jnp.dot(p_c.astype(BF16), vc_ref[rows, :].astype(BF16), preferred_element_type=F32)
        acc += jnp.dot(p_n.astype(BF16), vn_ref[:, sl].astype(BF16), preferred_element_type=F32)
        o_ref[h] = (acc / l).astype(o_ref.dtype)


def _fox_attention_sample(q, k_new, v_new, k_cache, v_cache, idx, c_cache, lf_new, n):
    m, d = q.shape
    _, b, rows, hd = k_cache.shape
    lc = rows // N_HEADS
    tok = pl.BlockSpec((n, d), lambda i: (i, 0))
    cache = pl.BlockSpec((None, None, rows, hd), lambda i: (idx, i, 0, 0))
    upper = jnp.triu(jnp.ones((n, n), BF16))
    return pl.pallas_call(
        _fox_sample_kernel,
        grid=(b,),
        in_specs=[tok, tok, tok, cache, cache,
                  pl.BlockSpec((1, N_HEADS, lc), lambda i: (i, 0, 0)),
                  pl.BlockSpec((1, N_HEADS, n), lambda i: (i, 0, 0)),
                  pl.BlockSpec((n, n), lambda i: (0, 0))],
        out_specs=pl.BlockSpec((N_HEADS, n, HEAD_DIM), lambda i: (0, i, 0)),
        out_shape=jax.ShapeDtypeStruct((N_HEADS, m, HEAD_DIM), BF16),
        compiler_params=_params("arbitrary"),
        name="fox_attention_sample",
    )(q, k_new, v_new, k_cache, v_cache, c_cache, lf_new, upper)


def _out_proj_kernel(a_ref, w_ref, x_ref, o_ref):
    a = jnp.concatenate([a_ref[h] for h in range(N_HEADS)], axis=1)
    o_ref[...] = x_ref[...] + jnp.dot(a, w_ref[...], preferred_element_type=F32)


def _out_proj(attn, w_o, x):
    m, d = x.shape
    tm = min(ROW_TILE, m)
    return pl.pallas_call(
        _out_proj_kernel,
        grid=(m // tm,),
        in_specs=[pl.BlockSpec((N_HEADS, tm, HEAD_DIM), lambda i: (0, i, 0)),
                  pl.BlockSpec((d, d), lambda i: (0, 0)),
                  pl.BlockSpec((tm, d), lambda i: (i, 0))],
        out_specs=pl.BlockSpec((tm, d), lambda i: (i, 0)),
        out_shape=jax.ShapeDtypeStruct((m, d), F32),
        compiler_params=_params("arbitrary"),
        name="out_proj",
    )(attn, w_o, x)


def _ffn_kernel(x_ref, g_ref, wg_ref, wu_ref, wd_ref, o_ref, xn_ref):
    @pl.when(pl.program_id(1) == 0)
    def _():
        x = x_ref[...]
        ms = jnp.mean(x * x, axis=-1, keepdims=True)
        xn_ref[...] = (x * lax.rsqrt(ms + EPS) * g_ref[...]).astype(BF16)
        o_ref[...] = x

    xn = xn_ref[...]
    gate = jnp.dot(xn, wg_ref[...], preferred_element_type=F32)
    up = jnp.dot(xn, wu_ref[...], preferred_element_type=F32)
    hidden = (gate * jax.nn.sigmoid(gate) * up).astype(BF16)
    o_ref[...] += jnp.dot(hidden, wd_ref[...], preferred_element_type=F32)


def _ffn(x, g_ffn, w_gate, w_up, w_down):
    m, d = x.shape
    f = w_gate.shape[1]
    tm = min(FFN_ROW_TILE, m)
    tf = FFN_TILE
    return pl.pallas_call(
        _ffn_kernel,
        grid=(m // tm, f // tf),
        in_specs=[pl.BlockSpec((tm, d), lambda i, j: (i, 0)),
                  pl.BlockSpec((1, d), lambda i, j: (0, 0)),
                  pl.BlockSpec((d, tf), lambda i, j: (0, j)),
                  pl.BlockSpec((d, tf), lambda i, j: (0, j)),
                  pl.BlockSpec((tf, d), lambda i, j: (j, 0))],
        out_specs=pl.BlockSpec((tm, d), lambda i, j: (i, 0)),
        out_shape=jax.ShapeDtypeStruct((m, d), F32),
        scratch_shapes=[pltpu.VMEM((tm, d), BF16)],
        compiler_params=_params("arbitrary", "arbitrary", vmem=BIG_VMEM_LIMIT),
        name="ffn",
    )(x, g_ffn.reshape(1, d), w_gate, w_up, w_down)


def kernel(x_prompt, x_sample, cache_a_k, cache_a_v, cache_b_k, cache_b_v, cache_b_logf,
           g_attn, w_qkv, g_q, g_k, w_o, rel_table, w_f, b_f, g_ffn, w_gate, w_up, w_down):
    batch, seq, d = x_prompt.shape
    dec_batch, n_new, _ = x_sample.shape
    depth = g_attn.shape[0]
    n_a, n_b = cache_a_k.shape[0], cache_b_k.shape[0]
    keep = min(WINDOW, seq)
    h, hd = N_HEADS, HEAD_DIM
    la, past = cache_a_k.shape[2], cache_b_k.shape[2]
    assert d == h * hd and keep == WINDOW and la == WINDOW

    w_f_b = w_f.astype(BF16)
    w_qkv_l = w_qkv[0].astype(BF16)

    cache_a_k = cache_a_k.reshape(n_a, dec_batch, la * h, hd)
    cache_a_v = cache_a_v.reshape(n_a, dec_batch, la * h, hd)
    cache_b_k = cache_b_k.reshape(n_b, dec_batch, past * h, hd)
    cache_b_v = cache_b_v.reshape(n_b, dec_batch, past * h, hd)

    yp = x_prompt.reshape(batch * seq, d)
    ys = x_sample.reshape(dec_batch * n_new, d)
    a_kv = [jnp.zeros((n_a, batch * keep * h, hd), F32) for _ in range(2)]
    b_layers, b_kv = [], None
    b_fp, a_ks, a_vs, b_ks, b_vs, b_fs = [], [], [], [], [], []

    for layer in range(depth):
        idx = layer // 2
        common = (g_attn[layer], w_qkv_l, g_q[layer], g_k[layer])
        cast = [(w_o, layer), (w_gate, layer), (w_up, layer), (w_down, layer)]
        if layer + 1 < depth:
            cast.append((w_qkv, layer + 1))
        reach = _qk_reach(g_q[layer], g_k[layer])
        if layer % 2 == 0:
            qb, kb, vb, *a_kv = _qkv_proj(
                yp, *common, sample=False, q_scale=SCALE * LOG2E, kv_last=keep, seq=seq,
                kv_slots=n_a, kv_slot=idx, kv_prev=a_kv)
            qs, ks, vs = _qkv_proj(ys, *common, sample=True)
            b_hi = jnp.max(rel_table[idx], axis=1) * LOG2E
            b_lo = jnp.min(rel_table[idx], axis=1) * LOG2E
            shifted = 2.0 * reach + (b_hi - b_lo) <= MAX_SLACK
            bias_p, bias_s = _rel_bias_blocks(rel_table[idx], n_new,
                                              jnp.where(shifted, reach + b_hi, 0.0))
            mp, *cast_w = _band_attention(qb, kb, vb, bias_p, shifted.astype(jnp.int32), batch, seq,
                                          cast=cast)
            ms = _band_attention_sample(qs, ks, vs, cache_a_k, cache_a_v, idx, bias_s, n_new)
            a_ks.append(ks)
            a_vs.append(vs)
        else:
            fw = (w_f_b[idx], b_f[idx])
            qb, kb, vb, k_all, v_all, lf = _qkv_proj(
                yp, *common, fw, sample=False, q_scale=SCALE * LOG2E, kv_slots=1)
            b_layers.append((k_all, v_all))
            qs, ks, vs, lfs = _qkv_proj(ys, *common, fw, sample=True)
            last_b = idx == n_b - 1
            mp, *more = _fox_attention(qb, kb, vb, _cumsum_lanes(lf.T, seq), reach, batch, seq,
                                       stack_kv=b_layers if last_b else (), cast=cast)
            b_kv, cast_w = more[:len(more) - len(cast)], more[len(more) - len(cast):]
            lf_cache = cache_b_logf[idx].transpose(0, 2, 1).reshape(dec_batch * h, past)
            c_cache = _cumsum_lanes(lf_cache, past).reshape(dec_batch, h, past)
            lf_new = lfs.reshape(dec_batch, n_new, h).transpose(0, 2, 1)
            ms = _fox_attention_sample(qs, ks, vs, cache_b_k, cache_b_v, idx, c_cache, lf_new, n_new)
            b_fp.append(lf.reshape(batch, seq, h))
            b_ks.append(ks)
            b_vs.append(vs)
            b_fs.append(lfs.reshape(dec_batch, n_new, h))
        w_o_l, w_gate_l, w_up_l, w_down_l = cast_w[:4]
        if layer + 1 < depth:
            w_qkv_l = cast_w[4]
        yp = _out_proj(mp, w_o_l, yp)
        ys = _out_proj(ms, w_o_l, ys)
        yp = _ffn(yp, g_ffn[layer], w_gate_l, w_up_l, w_down_l)
        ys = _ffn(ys, g_ffn[layer], w_gate_l, w_up_l, w_down_l)

    def new_kv(parts):
        return jnp.stack(parts).reshape(len(parts), dec_batch, n_new, h, hd)

    return (yp.reshape(batch, seq, d), ys.reshape(dec_batch, n_new, d),
            a_kv[0].reshape(n_a, batch, keep, h, hd), a_kv[1].reshape(n_a, batch, keep, h, hd),
            b_kv[0].reshape(n_b, batch, seq, h, hd), b_kv[1].reshape(n_b, batch, seq, h, hd),
            jnp.stack(b_fp), new_kv(a_ks), new_kv(a_vs), new_kv(b_ks), new_kv(b_vs), jnp.stack(b_fs))
```

```python
import functools
import math

import jax
import jax.numpy as jnp
from jax import lax
from jax.experimental import pallas as pl
from jax.experimental.pallas import tpu as pltpu

N_HEADS = 16
HEAD_DIM = 128
CHUNK = 64
N_LEFT_CHUNKS = 8
WINDOW = N_LEFT_CHUNKS * CHUNK
MAX_REL = 128
EPS = 1e-6
SCALE = HEAD_DIM ** -0.5
LOG2E = math.log2(math.e)
NEG = -1e30

BF16 = jnp.bfloat16
F32 = jnp.float32

LANES = 128
HEADS_PER_TILE = 4
ROW_TILE = 512
FFN_TILE = 512
FFN_ROW_TILE = 1024
BIG_VMEM_LIMIT = 62 * 1024 * 1024
BAND_Q = 256
BAND_STEP = 4096
FOX_BLOCK = 512
FOX_CHUNK = 2048
FOX_STEP = 2048
MAX_SLACK = 96.0
BOUND_MARGIN = 1.02
SCAN_BLOCK = 512
ROLL_WIDTH = 1024
VMEM_LIMIT = 56 * 1024 * 1024

_NT = (((1,), (1,)), ((), ()))


def _params(*sem, vmem=VMEM_LIMIT):
    return pltpu.CompilerParams(dimension_semantics=sem, vmem_limit_bytes=vmem)


def _log_sigmoid(z):
    return jnp.minimum(z, 0.0) - jnp.log1p(jnp.exp(-jnp.abs(z)))


def _qk_reach(g_q, g_k):
    return HEAD_DIM * jnp.max(jnp.abs(g_q)) * jnp.max(jnp.abs(g_k)) * (SCALE * LOG2E * BOUND_MARGIN)


def _split3(x):
    hi = x.astype(BF16).astype(F32)
    r1 = x - hi
    mid = r1.astype(BF16).astype(F32)
    lo = r1 - mid
    return jnp.concatenate([hi, mid, lo], axis=0).astype(BF16)


def _head_rows(head, n_rows):
    return pl.ds(head, n_rows, stride=N_HEADS)


def _cast_plan(cast, steps, step_index):
    in_specs, args, out_specs, out_shape = [], [], [], []
    for stack, layer in cast:
        _, rows, cols = stack.shape
        group = 1
        while rows * group % (steps * 16):
            group *= 2
            assert steps % group == 0
        rb = rows * group // steps
        in_specs.append(pl.BlockSpec(
            (None, rb, cols),
            lambda *ids, layer=layer, group=group: (layer, step_index(*ids) // group, 0)))
        args.append(stack)
        out_specs.append(pl.BlockSpec(
            (rb, cols), lambda *ids, group=group: (step_index(*ids) // group, 0)))
        out_shape.append(jax.ShapeDtypeStruct((rows, cols), BF16))
    return in_specs, args, out_specs, out_shape


def _cast_blocks(srcs, dsts):
    for src, dst in zip(srcs, dsts):
        dst[...] = src[...].astype(BF16)


def _qkv_kernel(*refs, forget, sample, q_scale, n_alias):
    x_ref, ga_ref, wq_ref, wk_ref, wv_ref, gq_ref, gk_ref = refs[:7]
    rest = list(refs[7:])
    if forget:
        wf_ref, bf_ref = rest[:2]
        rest = rest[2:]
    rest = rest[n_alias:]
    if sample:
        q_out, k_out, v_out = rest[:3]
        rest = rest[3:]
    else:
        qb_out, kb_out, vb_out, k_out, v_out = rest[:5]
        rest = rest[5:]
    if forget:
        lf_out = rest[0]
        rest = rest[1:]
    (xn_ref,) = rest
    j = pl.program_id(1)
    tm = x_ref.shape[0]

    @pl.when(j == 0)
    def _():
        x = x_ref[...]
        ms = jnp.mean(x * x, axis=-1, keepdims=True)
        xn = (x * lax.rsqrt(ms + EPS) * ga_ref[...]).astype(BF16)
        xn_ref[...] = xn
        if forget:
            z = jnp.dot(xn, wf_ref[...], preferred_element_type=F32) + bf_ref[...]
            lf_out[...] = _log_sigmoid(z)

    xn = xn_ref[...]
    q = jnp.dot(xn, wq_ref[...], preferred_element_type=F32)
    k = jnp.dot(xn, wk_ref[...], preferred_element_type=F32)
    v = jnp.dot(xn, wv_ref[...], preferred_element_type=F32)
    gq = gq_ref[...]
    gk = gk_ref[...]
    for hh in range(HEADS_PER_TILE):
        sl = slice(hh * HEAD_DIM, (hh + 1) * HEAD_DIM)
        qh = q[:, sl]
        kh = k[:, sl]
        vh = v[:, sl]
        qn = qh * lax.rsqrt(jnp.mean(qh * qh, axis=-1, keepdims=True) + EPS) * gq * q_scale
        kn = kh * lax.rsqrt(jnp.mean(kh * kh, axis=-1, keepdims=True) + EPS) * gk
        if sample:
            q_out[:, sl] = qn
            k_out[:, sl] = kn
            v_out[:, sl] = vh
        else:
            qb_out[hh] = qn.astype(BF16)
            kb_out[hh] = kn.astype(BF16)
            vb_out[hh] = vh.astype(BF16)
            rows = _head_rows(j * HEADS_PER_TILE + hh, tm)
            k_out[rows, :] = kn
            v_out[rows, :] = vh


def _qkv_proj(x, g_attn, w_qkv, g_q, g_k, forget_w=None, *, sample, q_scale=SCALE,
              kv_last=None, seq=None, kv_slots=None, kv_slot=0, kv_prev=None):
    m, d = x.shape
    tm = min(ROW_TILE, m)
    tn = HEADS_PER_TILE * HEAD_DIM
    nj = d // tn
    grid = (m // tm, nj)
    forget = forget_w is not None

    in_specs = [
        pl.BlockSpec((tm, d), lambda i, j: (i, 0)),
        pl.BlockSpec((1, d), lambda i, j: (0, 0)),
        pl.BlockSpec((d, tn), lambda i, j: (0, j)),
        pl.BlockSpec((d, tn), lambda i, j: (0, nj + j)),
        pl.BlockSpec((d, tn), lambda i, j: (0, 2 * nj + j)),
        pl.BlockSpec((1, HEAD_DIM), lambda i, j: (0, 0)),
        pl.BlockSpec((1, HEAD_DIM), lambda i, j: (0, 0)),
    ]
    args = [x, g_attn.reshape(1, d), w_qkv, w_qkv, w_qkv,
            g_q.reshape(1, HEAD_DIM), g_k.reshape(1, HEAD_DIM)]
    if forget:
        w_f, b_f = forget_w
        in_specs += [pl.BlockSpec((d, N_HEADS), lambda i, j: (0, 0)),
                     pl.BlockSpec((1, N_HEADS), lambda i, j: (0, 0))]
        args += [w_f, b_f.reshape(1, N_HEADS)]

    aliases = {}
    if sample:
        tok_spec = pl.BlockSpec((tm, tn), lambda i, j: (i, j))
        out_shape = [jax.ShapeDtypeStruct((m, d), F32)] * 3
        out_specs = [tok_spec] * 3
    else:
        hm_shape = jax.ShapeDtypeStruct((N_HEADS, m, HEAD_DIM), BF16)
        hm_spec = pl.BlockSpec((HEADS_PER_TILE, tm, HEAD_DIM), lambda i, j: (j, i, 0))
        if kv_last is None:
            kv_rows = m
            kv_spec = pl.BlockSpec((None, tm * N_HEADS, HEAD_DIM), lambda i, j: (kv_slot, i, 0))
        else:
            assert kv_last == tm and seq % tm == 0
            per_seq = seq // tm
            kv_rows = m // seq * kv_last
            kv_spec = pl.BlockSpec((None, tm * N_HEADS, HEAD_DIM),
                                   lambda i, j: (kv_slot, i // per_seq, 0))
        kv_shape = jax.ShapeDtypeStruct((kv_slots, kv_rows * N_HEADS, HEAD_DIM), F32)
        out_shape = [hm_shape] * 3 + [kv_shape] * 2
        out_specs = [hm_spec] * 3 + [kv_spec] * 2
        if kv_prev is not None:
            assert kv_prev[0].shape == kv_shape.shape
            aliases = {len(args): 3, len(args) + 1: 4}
            in_specs += [pl.BlockSpec(memory_space=pl.ANY)] * 2
            args += list(kv_prev)
    if forget:
        out_shape += [jax.ShapeDtypeStruct((m, N_HEADS), F32)]
        out_specs += [pl.BlockSpec((tm, N_HEADS), lambda i, j: (i, 0))]

    return pl.pallas_call(
        functools.partial(_qkv_kernel, forget=forget, sample=sample, q_scale=q_scale,
                          n_alias=len(aliases)),
        grid=grid,
        in_specs=in_specs,
        out_specs=out_specs,
        out_shape=out_shape,
        scratch_shapes=[pltpu.VMEM((tm, d), BF16)],
        input_output_aliases=aliases,
        compiler_params=_params("arbitrary", "arbitrary"),
        name="qkv_sample" if sample else "qkv_prompt",
    )(*args)


def _cumsum_kernel(x_ref, u_ref, o_ref, carry_ref, *, blocks_per_seq):
    @pl.when(pl.program_id(0) % blocks_per_seq == 0)
    def _():
        carry_ref[...] = jnp.zeros_like(carry_ref)

    x = x_ref[...]
    r, t = x.shape
    s = jnp.dot(_split3(x), u_ref[...], preferred_element_type=F32)
    c = s[:r] + s[r:2 * r] + s[2 * r:] + carry_ref[:, :1]
    o_ref[...] = c
    carry_ref[...] = jnp.broadcast_to(c[:, t - 1:t], carry_ref.shape)


def _cumsum_lanes(x, seq):
    r, l = x.shape
    t = min(SCAN_BLOCK, seq)
    upper = jnp.triu(jnp.ones((t, t), BF16))
    return pl.pallas_call(
        functools.partial(_cumsum_kernel, blocks_per_seq=seq // t),
        grid=(l // t,),
        in_specs=[pl.BlockSpec((r, t), lambda i: (0, i)),
                  pl.BlockSpec((t, t), lambda i: (0, 0))],
        out_specs=pl.BlockSpec((r, t), lambda i: (0, i)),
        out_shape=jax.ShapeDtypeStruct((r, l), F32),
        scratch_shapes=[pltpu.VMEM((r, LANES), F32)],
        compiler_params=_params("arbitrary"),
        name="cumsum_lanes",
    )(x, upper)


def _bias_kernel(shift_ref, xp_ref, xs_ref, bp_ref, bs_ref):
    def toeplitz(row, n_rows, lo, hi):
        base = jnp.broadcast_to(row, (8, ROLL_WIDTH))
        blocks = []
        for g in range(n_rows // 8):
            rolled = pltpu.roll(base, 8 * g, 1, stride=1, stride_axis=0)
            blocks.append(rolled[:, lo:hi])
        return jnp.concatenate(blocks, axis=0)

    row_c = lax.broadcasted_iota(jnp.int32, (BAND_Q, 3 * BAND_Q), 0) // CHUNK
    col = lax.broadcasted_iota(jnp.int32, (BAND_Q, 3 * BAND_Q), 1)
    col_c = (col % BAND_Q) // CHUNK
    visible = ((col >= BAND_Q) | (row_c <= col_c)) & ((col < 2 * BAND_Q) | (col_c <= row_c))
    shift = shift_ref[pl.program_id(0)]
    bp_ref[0] = jnp.where(visible, toeplitz(xp_ref[0], BAND_Q, BAND_Q, ROLL_WIDTH) * LOG2E - shift, NEG)
    bs_ref[0] = toeplitz(xs_ref[0], bs_ref.shape[1], LANES, LANES + bs_ref.shape[2])


def _rel_bias_blocks(table, n_new, shift):
    assert 2 * BAND_Q == WINDOW and 3 * BAND_Q + BAND_Q == ROLL_WIDTH
    rev = table[:, ::-1]
    n_rel = table.shape[1]
    lead_p = 3 * BAND_Q - MAX_REL
    lead_s = WINDOW + LANES - MAX_REL
    row_p = jnp.pad(rev, ((0, 0), (lead_p, ROLL_WIDTH - n_rel - lead_p)), mode="edge")
    row_s = jnp.pad(rev, ((0, 0), (lead_s, ROLL_WIDTH - n_rel - lead_s)), mode="edge")
    h = table.shape[0]
    ws = WINDOW + LANES
    return pl.pallas_call(
        _bias_kernel,
        grid=(h,),
        in_specs=[pl.BlockSpec(memory_space=pltpu.SMEM),
                  pl.BlockSpec((1, 1, ROLL_WIDTH), lambda i: (i, 0, 0)),
                  pl.BlockSpec((1, 1, ROLL_WIDTH), lambda i: (i, 0, 0))],
        out_specs=[pl.BlockSpec((1, BAND_Q, 3 * BAND_Q), lambda i: (i, 0, 0)),
                   pl.BlockSpec((1, n_new, ws), lambda i: (i, 0, 0))],
        out_shape=[jax.ShapeDtypeStruct((h, BAND_Q, 3 * BAND_Q), F32),
                   jax.ShapeDtypeStruct((h, n_new, ws), F32)],
        compiler_params=_params("arbitrary"),
        name="rel_bias",
    )(shift, row_p.reshape(h, 1, ROLL_WIDTH), row_s.reshape(h, 1, ROLL_WIDTH))


def _band_kernel(shifted_ref, q_ref, k_ref, v_ref, b_ref, *rest, batch, n_cast):
    o_ref = rest[n_cast]
    _cast_blocks(rest[:n_cast], rest[n_cast + 1:])
    tq = BAND_Q
    n_sub = q_ref.shape[1] // tq
    step = pl.program_id(1)

    def attend(shifted):
        for sub in range(n_sub):
            i = step * n_sub + sub
            q = q_ref[0, sub * tq:(sub + 1) * tq, :]
            s_parts, v_parts = [], []
            for t in range(3):
                start = pl.multiple_of(jnp.maximum(i - 2 + t, 0) * tq, tq)
                kb = k_ref[0, pl.ds(start, tq), :]
                v_parts.append(v_ref[0, pl.ds(start, tq), :])
                s = lax.dot_general(q, kb, _NT, preferred_element_type=F32)
                s = s + b_ref[0, :, t * tq:(t + 1) * tq]
                if sub + t < 2:
                    s = s + jnp.where(i + t >= 2, 0.0, NEG)
                s_parts.append(s)
            s = jnp.concatenate(s_parts, axis=1)
            if not shifted:
                s = s - jnp.max(s, axis=-1, keepdims=True)
            p = jnp.exp2(s)
            l = jnp.sum(p, axis=-1, keepdims=True)
            pb = p.astype(BF16)
            acc = jnp.dot(pb[:, :tq], v_parts[0], preferred_element_type=F32)
            acc += jnp.dot(pb[:, tq:2 * tq], v_parts[1], preferred_element_type=F32)
            acc += jnp.dot(pb[:, 2 * tq:], v_parts[2], preferred_element_type=F32)
            o_ref[0, sub * tq:(sub + 1) * tq, :] = (acc / l).astype(o_ref.dtype)

    shifted = shifted_ref[pl.program_id(0) // batch] != 0
    pl.when(shifted)(lambda: attend(True))
    pl.when(jnp.logical_not(shifted))(lambda: attend(False))


def _band_attention(qb, kb, vb, bias, shifted, batch, seq, cast=()):
    h, m, hd = qb.shape
    step = min(BAND_STEP, seq)
    nq = seq // step
    c_in, c_args, c_out, c_shape = _cast_plan(cast, h * batch * nq, lambda g, i: g * nq + i)
    return pl.pallas_call(
        functools.partial(_band_kernel, batch=batch, n_cast=len(cast)),
        grid=(h * batch, nq),
        in_specs=[pl.BlockSpec(memory_space=pltpu.SMEM),
                  pl.BlockSpec((1, step, hd), lambda g, i: (g // batch, (g % batch) * nq + i, 0)),
                  pl.BlockSpec((1, seq, hd), lambda g, i: (g // batch, g % batch, 0)),
                  pl.BlockSpec((1, seq, hd), lambda g, i: (g // batch, g % batch, 0)),
                  pl.BlockSpec((1, BAND_Q, 3 * BAND_Q), lambda g, i: (g // batch, 0, 0))] + c_in,
        out_specs=[pl.BlockSpec((1, step, hd),
                                lambda g, i: (g // batch, (g % batch) * nq + i, 0))] + c_out,
        out_shape=[jax.ShapeDtypeStruct((h, m, hd), BF16)] + c_shape,
        compiler_params=_params("arbitrary", "arbitrary"),
        name="band_attention",
    )(shifted, qb, kb, vb, bias, *c_args)


def _band_sample_kernel(q_ref, kn_ref, vn_ref, kc_ref, vc_ref, b_ref, o_ref):
    n = q_ref.shape[0]
    lc = kc_ref.shape[0] // N_HEADS
    for h in range(N_HEADS):
        sl = slice(h * HEAD_DIM, (h + 1) * HEAD_DIM)
        rows = _head_rows(h, lc)
        q = q_ref[:, sl].astype(BF16)
        s_c = lax.dot_general(q, kc_ref[rows, :].astype(BF16), _NT, preferred_element_type=F32)
        s_n = lax.dot_general(q, kn_ref[:, sl].astype(BF16), _NT, preferred_element_type=F32)
        s_c = s_c + b_ref[h, :, :lc]
        s_n = s_n + b_ref[h, :, lc:lc + n]
        m = jnp.maximum(jnp.max(s_c, axis=-1, keepdims=True), jnp.max(s_n, axis=-1, keepdims=True))
        p_c = jnp.exp(s_c - m)
        p_n = jnp.exp(s_n - m)
        l = jnp.sum(p_c, axis=-1, keepdims=True) + jnp.sum(p_n, axis=-1, keepdims=True)
        acc = jnp.dot(p_c.astype(BF16), vc_ref[rows, :].astype(BF16), preferred_element_type=F32)
        acc += jnp.dot(p_n.astype(BF16), vn_ref[:, sl].astype(BF16), preferred_element_type=F32)
        o_ref[h] = (acc / l).astype(o_ref.dtype)


def _band_attention_sample(q, k_new, v_new, k_cache, v_cache, idx, bias, n):
    m, d = q.shape
    _, b, rows, hd = k_cache.shape
    tok = pl.BlockSpec((n, d), lambda i: (i, 0))
    cache = pl.BlockSpec((None, None, rows, hd), lambda i: (idx, i, 0, 0))
    return pl.pallas_call(
        _band_sample_kernel,
        grid=(b,),
        in_specs=[tok, tok, tok, cache, cache,
                  pl.BlockSpec(bias.shape, lambda i: (0, 0, 0))],
        out_specs=pl.BlockSpec((N_HEADS, n, HEAD_DIM), lambda i: (0, i, 0)),
        out_shape=jax.ShapeDtypeStruct((N_HEADS, m, HEAD_DIM), BF16),
        compiler_params=_params("arbitrary"),
        name="band_attention_sample",
    )(q, k_new, v_new, k_cache, v_cache, bias)


def _online_softmax_step(s, carry, v):
    m, l, acc = carry
    m_new = jnp.maximum(m, jnp.max(s, axis=-1, keepdims=True))
    alpha = jnp.exp2(m - m_new)
    p = jnp.exp2(s - m_new)
    l = alpha * l + jnp.sum(p, axis=-1, keepdims=True)
    acc = alpha * acc + jnp.dot(p.astype(BF16), v, preferred_element_type=F32)
    return m_new, l, acc


def _fox_kernel(reach_ref, q_ref, k_ref, v_ref, c_ref, *rest, n_stack, n_cast):
    layer_k, layer_v = rest[:n_stack], rest[n_stack:2 * n_stack]
    rest = rest[2 * n_stack:]
    o_ref = rest[n_cast]
    _cast_blocks(rest[:n_cast], rest[len(rest) - n_cast:])
    if n_stack:
        k_stack, v_stack = rest[n_cast + 1:n_cast + 3]
        for t in range(n_stack):
            k_stack[t] = layer_k[t][0]
            v_stack[t] = layer_v[t][0]

    i = pl.program_id(1)
    tq, hd = q_ref.shape[1], q_ref.shape[2]
    blk = min(FOX_BLOCK, tq)
    tk = min(FOX_CHUNK, tq)
    n, per = tq // blk, tk // blk
    qs = [q_ref[0, a * blk:(a + 1) * blk, :] for a in range(n)]
    reach = reach_ref[0]
    base = pl.multiple_of(i * tq, tq)
    causal = (lax.broadcasted_iota(jnp.int32, (blk, blk), 1)
              <= lax.broadcasted_iota(jnp.int32, (blk, blk), 0))

    def c_rows(first_block, count):
        return [c_ref[0, pl.ds(first_block + t, 1), :] * LOG2E for t in range(count)]

    def c_min_blocks(first_block, count):
        mins = [jnp.min(c, axis=-1, keepdims=True) for c in c_rows(first_block, count)]
        return functools.reduce(jnp.minimum, mins)

    def scores(q, start, size):
        k = k_ref[0, pl.ds(start, size), :]
        return (lax.dot_general(q, k, _NT, preferred_element_type=F32),
                v_ref[0, pl.ds(start, size), :])

    def minus_rows(s, rows):
        return jnp.concatenate([s[:, t * blk:(t + 1) * blk] - r for t, r in enumerate(rows)], axis=1)

    def mask_own(s, a):
        own = jnp.where(causal, s[:, a * blk:], NEG)
        return own if a == 0 else jnp.concatenate([s[:, :a * blk], own], axis=1)

    def finish(a, l, acc):
        o_ref[0, a * blk:(a + 1) * blk, :] = (acc / l).astype(o_ref.dtype)

    def attend_exact():
        def full_chunk(j, carry):
            start = pl.multiple_of(j * tk, tk)
            out = []
            for a in range(n):
                s, v = scores(qs[a], start, tk)
                out.append(_online_softmax_step(minus_rows(s, c_rows(j * per, per)), carry[a], v))
            return tuple(out)

        init = (jnp.full((blk, 1), NEG, F32), jnp.zeros((blk, 1), F32), jnp.zeros((blk, hd), F32))
        carry = lax.fori_loop(0, i * (tq // tk), full_chunk, (init,) * n)
        for a in range(n):
            s, v = scores(qs[a], base, (a + 1) * blk)
            s = mask_own(minus_rows(s, c_rows(i * n, a + 1)), a)
            _, l, acc = _online_softmax_step(s, carry[a], v)
            finish(a, l, acc)

    def attend_bounded():
        def update(s, m_new, carry, v):
            m, l, acc = carry
            alpha = jnp.exp2(m - m_new)
            p = jnp.exp2(s)
            l = alpha * l + jnp.sum(p, axis=-1, keepdims=True)
            return m_new, l, alpha * acc + jnp.dot(p.astype(BF16), v, preferred_element_type=F32)

        def full_chunk(j, carry):
            start = pl.multiple_of(j * tk, tk)
            rows = c_rows(j * per, per)
            m_new = jnp.maximum(carry[0][0], reach - c_min_blocks(j * per, per))
            shifted = [r + m_new for r in rows]
            out = []
            for a in range(n):
                s, v = scores(qs[a], start, tk)
                out.append(update(minus_rows(s, shifted), m_new, carry[a], v))
            return tuple(out)

        init = (jnp.full((1, 1), NEG, F32), jnp.zeros((blk, 1), F32), jnp.zeros((blk, hd), F32))
        carry = lax.fori_loop(0, i * (tq // tk), full_chunk, (init,) * n)
        for a in range(n):
            rows = c_rows(i * n, a + 1)
            c_vis = jnp.min(jnp.where(causal, rows[-1], -NEG), axis=-1, keepdims=True)
            c_seen = c_vis if a == 0 else jnp.minimum(c_vis, c_min_blocks(i * n, a))
            m_new = jnp.maximum(carry[a][0], reach - c_seen)
            s, v = scores(qs[a], base, (a + 1) * blk)
            s = mask_own(minus_rows(s, rows) - m_new, a)
            _, l, acc = update(s, m_new, carry[a], v)
            finish(a, l, acc)

    use_bound = 2.0 * reach <= MAX_SLACK
    pl.when(use_bound)(attend_bounded)
    pl.when(jnp.logical_not(use_bound))(attend_exact)


def _fox_attention(qb, kb, vb, c_rows, reach, batch, seq, stack_kv=(), cast=()):
    h, m, hd = qb.shape
    t = min(FOX_STEP, seq)
    nq = seq // t
    blk = min(FOX_BLOCK, t)
    c_blocks = c_rows.reshape(h, m // blk, blk)
    in_specs = [pl.BlockSpec(memory_space=pltpu.SMEM),
                pl.BlockSpec((1, t, hd), lambda g, i: (g // batch, (g % batch) * nq + i, 0)),
                pl.BlockSpec((1, seq, hd), lambda g, i: (g // batch, g % batch, 0)),
                pl.BlockSpec((1, seq, hd), lambda g, i: (g // batch, g % batch, 0)),
                pl.BlockSpec((1, seq // blk, blk), lambda g, i: (g // batch, g % batch, 0))]
    out_specs = [pl.BlockSpec((1, t, hd), lambda g, i: (g // batch, (g % batch) * nq + i, 0))]
    out_shape = [jax.ShapeDtypeStruct((h, m, hd), BF16)]
    n_stack = len(stack_kv)
    extra = []
    if n_stack:
        rows = stack_kv[0][0].shape[1]
        steps = h * batch * nq
        assert rows % (steps * 8) == 0
        per = rows // steps
        in_specs += [pl.BlockSpec((1, per, hd), lambda g, i: (0, g * nq + i, 0))] * (2 * n_stack)
        extra = [kv[0] for kv in stack_kv] + [kv[1] for kv in stack_kv]
        out_specs += [pl.BlockSpec((n_stack, per, hd), lambda g, i: (0, g * nq + i, 0))] * 2
        out_shape += [jax.ShapeDtypeStruct((n_stack, rows, hd), F32)] * 2
    c_in, c_args, c_out, c_shape = _cast_plan(cast, h * batch * nq, lambda g, i: g * nq + i)
    return pl.pallas_call(
        functools.partial(_fox_kernel, n_stack=n_stack, n_cast=len(cast)),
        grid=(h * batch, nq),
        in_specs=in_specs + c_in,
        out_specs=out_specs + c_out,
        out_shape=out_shape + c_shape,
        compiler_params=_params("arbitrary", "arbitrary", vmem=BIG_VMEM_LIMIT),
        name="fox_attention",
    )(reach.reshape(1), qb, kb, vb, c_blocks, *extra, *c_args)


def _fox_sample_kernel(q_ref, kn_ref, vn_ref, kc_ref, vc_ref, cc_ref, lfn_ref, u_ref, o_ref):
    n = q_ref.shape[0]
    c_cache = cc_ref[0]
    lc = c_cache.shape[1]
    s3 = jnp.dot(_split3(lfn_ref[0]), u_ref[...], preferred_element_type=F32)
    c_new = s3[:N_HEADS] + s3[N_HEADS:2 * N_HEADS] + s3[2 * N_HEADS:] + c_cache[:, lc - 1:lc]
    row = lax.broadcasted_iota(jnp.int32, (n, n), 0)
    col = lax.broadcasted_iota(jnp.int32, (n, n), 1)
    for h in range(N_HEADS):
        sl = slice(h * HEAD_DIM, (h + 1) * HEAD_DIM)
        rows = _head_rows(h, lc)
        q = q_ref[:, sl].astype(BF16)
        s_c = lax.dot_general(q, kc_ref[rows, :].astype(BF16), _NT, preferred_element_type=F32)
        s_n = lax.dot_general(q, kn_ref[:, sl].astype(BF16), _NT, preferred_element_type=F32)
        s_c = s_c - c_cache[h:h + 1, :]
        s_n = jnp.where(col <= row, s_n - c_new[h:h + 1, :], NEG)
        m = jnp.maximum(jnp.max(s_c, axis=-1, keepdims=True), jnp.max(s_n, axis=-1, keepdims=True))
        p_c = jnp.exp(s_c - m)
        p_n = jnp.exp(s_n - m)
        l = jnp.sum(p_c, axis=-1, keepdims=True) + jnp.sum(p_n, axis=-1, keepdims=True)
        acc = jnp.dot(p_c.astype(BF16), vc_ref[rows, :].astype(BF16), preferred_element_type=F32)
        acc += jnp.dot(p_n.astype(BF16), vn_ref[:, sl].astype(BF16), preferred_element_type=F32)
        o_ref[h] = (acc / l).astype(o_ref.dtype)


def _fox_attention_sample(q, k_new, v_new, k_cache, v_cache, idx, c_cache, lf_new, n):
    m, d = q.shape
    _, b, rows, hd = k_cache.shape
    lc = rows // N_HEADS
    tok = pl.BlockSpec((n, d), lambda i: (i, 0))
    cache = pl.BlockSpec((None, None, rows, hd), lambda i: (idx, i, 0, 0))
    upper = jnp.triu(jnp.ones((n, n), BF16))
    return pl.pallas_call(
        _fox_sample_kernel,
        grid=(b,),
        in_specs=[tok, tok, tok, cache, cache,
                  pl.BlockSpec((1, N_HEADS, lc), lambda i: (i, 0, 0)),
                  pl.BlockSpec((1, N_HEADS, n), lambda i: (i, 0, 0)),
                  pl.BlockSpec((n, n), lambda i: (0, 0))],
        out_specs=pl.BlockSpec((N_HEADS, n, HEAD_DIM), lambda i: (0, i, 0)),
        out_shape=jax.ShapeDtypeStruct((N_HEADS, m, HEAD_DIM), BF16),
        compiler_params=_params("arbitrary"),
        name="fox_attention_sample",
    )(q, k_new, v_new, k_cache, v_cache, c_cache, lf_new, upper)


def _out_proj_kernel(a_ref, w_ref, x_ref, o_ref):
    a = jnp.concatenate([a_ref[h] for h in range(N_HEADS)], axis=1)
    o_ref[...] = x_ref[...] + jnp.dot(a, w_ref[...], preferred_element_type=F32)


def _out_proj(attn, w_o, x):
    m, d = x.shape
    tm = min(ROW_TILE, m)
    return pl.pallas_call(
        _out_proj_kernel,
        grid=(m // tm,),
        in_specs=[pl.BlockSpec((N_HEADS, tm, HEAD_DIM), lambda i: (0, i, 0)),
                  pl.BlockSpec((d, d), lambda i: (0, 0)),
                  pl.BlockSpec((tm, d), lambda i: (i, 0))],
        out_specs=pl.BlockSpec((tm, d), lambda i: (i, 0)),
        out_shape=jax.ShapeDtypeStruct((m, d), F32),
        compiler_params=_params("arbitrary"),
        name="out_proj",
    )(attn, w_o, x)


def _ffn_kernel(x_ref, g_ref, wg_ref, wu_ref, wd_ref, o_ref, xn_ref):
    @pl.when(pl.program_id(1) == 0)
    def _():
        x = x_ref[...]
        ms = jnp.mean(x * x, axis=-1, keepdims=True)
        xn_ref[...] = (x * lax.rsqrt(ms + EPS) * g_ref[...]).astype(BF16)
        o_ref[...] = x

    xn = xn_ref[...]
    gate = jnp.dot(xn, wg_ref[...], preferred_element_type=F32)
    up = jnp.dot(xn, wu_ref[...], preferred_element_type=F32)
    hidden = (gate * jax.nn.sigmoid(gate) * up).astype(BF16)
    o_ref[...] += jnp.dot(hidden, wd_ref[...], preferred_element_type=F32)


def _ffn(x, g_ffn, w_gate, w_up, w_down):
    m, d = x.shape
    f = w_gate.shape[1]
    tm = min(FFN_ROW_TILE, m)
    tf = FFN_TILE
    return pl.pallas_call(
        _ffn_kernel,
        grid=(m // tm, f // tf),
        in_specs=[pl.BlockSpec((tm, d), lambda i, j: (i, 0)),
                  pl.BlockSpec((1, d), lambda i, j: (0, 0)),
                  pl.BlockSpec((d, tf), lambda i, j: (0, j)),
                  pl.BlockSpec((d, tf), lambda i, j: (0, j)),
                  pl.BlockSpec((tf, d), lambda i, j: (j, 0))],
        out_specs=pl.BlockSpec((tm, d), lambda i, j: (i, 0)),
        out_shape=jax.ShapeDtypeStruct((m, d), F32),
        scratch_shapes=[pltpu.VMEM((tm, d), BF16)],
        compiler_params=_params("arbitrary", "arbitrary", vmem=BIG_VMEM_LIMIT),
        name="ffn",
    )(x, g_ffn.reshape(1, d), w_gate, w_up, w_down)


def kernel(x_prompt, x_sample, cache_a_k, cache_a_v, cache_b_k, cache_b_v, cache_b_logf,
           g_attn, w_qkv, g_q, g_k, w_o, rel_table, w_f, b_f, g_ffn, w_gate, w_up, w_down):
    batch, seq, d = x_prompt.shape
    dec_batch, n_new, _ = x_sample.shape
    depth = g_attn.shape[0]
    n_a, n_b = cache_a_k.shape[0], cache_b_k.shape[0]
    keep = min(WINDOW, seq)
    h, hd = N_HEADS, HEAD_DIM
    la, past = cache_a_k.shape[2], cache_b_k.shape[2]
    assert d == h * hd and keep == WINDOW and la == WINDOW

    w_f_b = w_f.astype(BF16)
    w_qkv_l = w_qkv[0].astype(BF16)

    cache_a_k = cache_a_k.reshape(n_a, dec_batch, la * h, hd)
    cache_a_v = cache_a_v.reshape(n_a, dec_batch, la * h, hd)
    cache_b_k = cache_b_k.reshape(n_b, dec_batch, past * h, hd)
    cache_b_v = cache_b_v.reshape(n_b, dec_batch, past * h, hd)

    yp = x_prompt.reshape(batch * seq, d)
    ys = x_sample.reshape(dec_batch * n_new, d)
    a_kv = [jnp.zeros((n_a, batch * keep * h, hd), F32) for _ in range(2)]
    b_layers, b_kv = [], None
    b_fp, a_ks, a_vs, b_ks, b_vs, b_fs = [], [], [], [], [], []

    for layer in range(depth):
        idx = layer // 2
        common = (g_attn[layer], w_qkv_l, g_q[layer], g_k[layer])
        cast = [(w_o, layer), (w_gate, layer), (w_up, layer), (w_down, layer)]
        if layer + 1 < depth:
            cast.append((w_qkv, layer + 1))
        reach = _qk_reach(g_q[layer], g_k[layer])
        if layer % 2 == 0:
            qb, kb, vb, *a_kv = _qkv_proj(
                yp, *common, sample=False, q_scale=SCALE * LOG2E, kv_last=keep, seq=seq,
                kv_slots=n_a, kv_slot=idx, kv_prev=a_kv)
            qs, ks, vs = _qkv_proj(ys, *common, sample=True)
            b_hi = jnp.max(rel_table[idx], axis=1) * LOG2E
            b_lo = jnp.min(rel_table[idx], axis=1) * LOG2E
            shifted = 2.0 * reach + (b_hi - b_lo) <= MAX_SLACK
            bias_p, bias_s = _rel_bias_blocks(rel_table[idx], n_new,
                                              jnp.where(shifted, reach + b_hi, 0.0))
            mp, *cast_w = _band_attention(qb, kb, vb, bias_p, shifted.astype(jnp.int32), batch, seq,
                                          cast=cast)
            ms = _band_attention_sample(qs, ks, vs, cache_a_k, cache_a_v, idx, bias_s, n_new)
            a_ks.append(ks)
            a_vs.append(vs)
        else:
            fw = (w_f_b[idx], b_f[idx])
            qb, kb, vb, k_all, v_all, lf = _qkv_proj(
                yp, *common, fw, sample=False, q_scale=SCALE * LOG2E, kv_slots=1)
            b_layers.append((k_all, v_all))
            qs, ks, vs, lfs = _qkv_proj(ys, *common, fw, sample=True)
            last_b = idx == n_b - 1
            mp, *more = _fox_attention(qb, kb, vb, _cumsum_lanes(lf.T, seq), reach, batch, seq,
                                       stack_kv=b_layers if last_b else (), cast=cast)
            b_kv, cast_w = more[:len(more) - len(cast)], more[len(more) - len(cast):]
            lf_cache = cache_b_logf[idx].transpose(0, 2, 1).reshape(dec_batch * h, past)
            c_cache = _cumsum_lanes(lf_cache, past).reshape(dec_batch, h, past)
            lf_new = lfs.reshape(dec_batch, n_new, h).transpose(0, 2, 1)
            ms = _fox_attention_sample(qs, ks, vs, cache_b_k, cache_b_v, idx, c_cache, lf_new, n_new)
            b_fp.append(lf.reshape(batch, seq, h))
            b_ks.append(ks)
            b_vs.append(vs)
            b_fs.append(lfs.reshape(dec_batch, n_new, h))
        w_o_l, w_gate_l, w_up_l, w_down_l = cast_w[:4]
        if layer + 1 < depth:
            w_qkv_l = cast_w[4]
        yp = _out_proj(mp, w_o_l, yp)
        ys = _out_proj(ms, w_o_l, ys)
        yp = _ffn(yp, g_ffn[layer], w_gate_l, w_up_l, w_down_l)
        ys = _ffn(ys, g_ffn[layer], w_gate_l, w_up_l, w_down_l)

    def new_kv(parts):
        return jnp.stack(parts).reshape(len(parts), dec_batch, n_new, h, hd)

    return (yp.reshape(batch, seq, d), ys.reshape(dec_batch, n_new, d),
            a_kv[0].reshape(n_a, batch, keep, h, hd), a_kv[1].reshape(n_a, batch, keep, h, hd),
            b_kv[0].reshape(n_b, batch, seq, h, hd), b_kv[1].reshape(n_b, batch, seq, h, hd),
            jnp.stack(b_fp), new_kv(a_ks), new_kv(a_vs), new_kv(b_ks), new_kv(b_vs), jnp.stack(b_fs))
```

```python
import functools
import math

import jax
import jax.numpy as jnp
from jax import lax
from jax.experimental import pallas as pl
from jax.experimental.pallas import tpu as pltpu

N_HEADS = 16
HEAD_DIM = 128
CHUNK = 64
N_LEFT_CHUNKS = 8
WINDOW = N_LEFT_CHUNKS * CHUNK
MAX_REL = 128
EPS = 1e-6
SCALE = HEAD_DIM ** -0.5
LOG2E = math.log2(math.e)
NEG = -1e30

BF16 = jnp.bfloat16
F32 = jnp.float32

LANES = 128
HEADS_PER_TILE = 4
ROW_TILE = 512
FFN_TILE = 512
FFN_ROW_TILE = 1024
BIG_VMEM_LIMIT = 62 * 1024 * 1024
BAND_Q = 256
BAND_STEP = 4096
FOX_BLOCK = 512
FOX_CHUNK = 2048
FOX_STEP = 2048
MAX_SLACK = 96.0
BOUND_MARGIN = 1.02
SCAN_BLOCK = 512
ROLL_WIDTH = 1024
VMEM_LIMIT = 56 * 1024 * 1024

_NT = (((1,), (1,)), ((), ()))


def _params(*sem, vmem=VMEM_LIMIT):
    return pltpu.CompilerParams(dimension_semantics=sem, vmem_limit_bytes=vmem)


def _log_sigmoid(z):
    return jnp.minimum(z, 0.0) - jnp.log1p(jnp.exp(-jnp.abs(z)))


def _qk_reach(g_q, g_k):
    return HEAD_DIM * jnp.max(jnp.abs(g_q)) * jnp.max(jnp.abs(g_k)) * (SCALE * LOG2E * BOUND_MARGIN)


def _split3(x):
    hi = x.astype(BF16).astype(F32)
    r1 = x - hi
    mid = r1.astype(BF16).astype(F32)
    lo = r1 - mid
    return jnp.concatenate([hi, mid, lo], axis=0).astype(BF16)


def _head_rows(head, n_rows):
    return pl.ds(head, n_rows, stride=N_HEADS)


def _cast_plan(cast, steps, step_index):
    in_specs, args, out_specs, out_shape = [], [], [], []
    for stack, layer in cast:
        _, rows, cols = stack.shape
        group = 1
        while rows * group % (steps * 16):
            group *= 2
            assert steps % group == 0
        rb = rows * group // steps
        in_specs.append(pl.BlockSpec(
            (None, rb, cols),
            lambda *ids, layer=layer, group=group: (layer, step_index(*ids) // group, 0)))
        args.append(stack)
        out_specs.append(pl.BlockSpec(
            (rb, cols), lambda *ids, group=group: (step_index(*ids) // group, 0)))
        out_shape.append(jax.ShapeDtypeStruct((rows, cols), BF16))
    return in_specs, args, out_specs, out_shape


def _cast_blocks(srcs, dsts):
    for src, dst in zip(srcs, dsts):
        dst[...] = src[...].astype(BF16)


def _qkv_kernel(*refs, forget, sample, q_scale, n_alias):
    x_ref, ga_ref, wq_ref, wk_ref, wv_ref, gq_ref, gk_ref = refs[:7]
    rest = list(refs[7:])
    if forget:
        wf_ref, bf_ref = rest[:2]
        rest = rest[2:]
    rest = rest[n_alias:]
    if sample:
        q_out, k_out, v_out = rest[:3]
        rest = rest[3:]
    else:
        qb_out, kb_out, vb_out, k_out, v_out = rest[:5]
        rest = rest[5:]
    if forget:
        lf_out = rest[0]
        rest = rest[1:]
    (xn_ref,) = rest
    j = pl.program_id(1)
    tm = x_ref.shape[0]

    @pl.when(j == 0)
    def _():
        x = x_ref[...]
        ms = jnp.mean(x * x, axis=-1, keepdims=True)
        xn = (x * lax.rsqrt(ms + EPS) * ga_ref[...]).astype(BF16)
        xn_ref[...] = xn
        if forget:
            z = jnp.dot(xn, wf_ref[...], preferred_element_type=F32) + bf_ref[...]
            lf_out[...] = _log_sigmoid(z)

    xn = xn_ref[...]
    q = jnp.dot(xn, wq_ref[...], preferred_element_type=F32)
    k = jnp.dot(xn, wk_ref[...], preferred_element_type=F32)
    v = jnp.dot(xn, wv_ref[...], preferred_element_type=F32)
    gq = gq_ref[...]
    gk = gk_ref[...]
    for hh in range(HEADS_PER_TILE):
        sl = slice(hh * HEAD_DIM, (hh + 1) * HEAD_DIM)
        qh = q[:, sl]
        kh = k[:, sl]
        vh = v[:, sl]
        qn = qh * lax.rsqrt(jnp.mean(qh * qh, axis=-1, keepdims=True) + EPS) * gq * q_scale
        kn = kh * lax.rsqrt(jnp.mean(kh * kh, axis=-1, keepdims=True) + EPS) * gk
        if sample:
            q_out[:, sl] = qn
            k_out[:, sl] = kn
            v_out[:, sl] = vh
        else:
            qb_out[hh] = qn.astype(BF16)
            kb_out[hh] = kn.astype(BF16)
            vb_out[hh] = vh.astype(BF16)
            rows = _head_rows(j * HEADS_PER_TILE + hh, tm)
            k_out[rows, :] = kn
            v_out[rows, :] = vh


def _qkv_proj(x, g_attn, w_qkv, g_q, g_k, forget_w=None, *, sample, q_scale=SCALE,
              kv_last=None, seq=None, kv_slots=None, kv_slot=0, kv_prev=None):
    m, d = x.shape
    tm = min(ROW_TILE, m)
    tn = HEADS_PER_TILE * HEAD_DIM
    nj = d // tn
    grid = (m // tm, nj)
    forget = forget_w is not None

    in_specs = [
        pl.BlockSpec((tm, d), lambda i, j: (i, 0)),
        pl.BlockSpec((1, d), lambda i, j: (0, 0)),
        pl.BlockSpec((d, tn), lambda i, j: (0, j)),
        pl.BlockSpec((d, tn), lambda i, j: (0, nj + j)),
        pl.BlockSpec((d, tn), lambda i, j: (0, 2 * nj + j)),
        pl.BlockSpec((1, HEAD_DIM), lambda i, j: (0, 0)),
        pl.BlockSpec((1, HEAD_DIM), lambda i, j: (0, 0)),
    ]
    args = [x, g_attn.reshape(1, d), w_qkv, w_qkv, w_qkv,
            g_q.reshape(1, HEAD_DIM), g_k.reshape(1, HEAD_DIM)]
    if forget:
        w_f, b_f = forget_w
        in_specs += [pl.BlockSpec((d, N_HEADS), lambda i, j: (0, 0)),
                     pl.BlockSpec((1, N_HEADS), lambda i, j: (0, 0))]
        args += [w_f, b_f.reshape(1, N_HEADS)]

    aliases = {}
    if sample:
        tok_spec = pl.BlockSpec((tm, tn), lambda i, j: (i, j))
        out_shape = [jax.ShapeDtypeStruct((m, d), F32)] * 3
        out_specs = [tok_spec] * 3
    else:
        hm_shape = jax.ShapeDtypeStruct((N_HEADS, m, HEAD_DIM), BF16)
        hm_spec = pl.BlockSpec((HEADS_PER_TILE, tm, HEAD_DIM), lambda i, j: (j, i, 0))
        if kv_last is None:
            kv_rows = m
            kv_spec = pl.BlockSpec((None, tm * N_HEADS, HEAD_DIM), lambda i, j: (kv_slot, i, 0))
        else:
            assert kv_last == tm and seq % tm == 0
            per_seq = seq // tm
            kv_rows = m // seq * kv_last
            kv_spec = pl.BlockSpec((None, tm * N_HEADS, HEAD_DIM),
                                   lambda i, j: (kv_slot, i // per_seq, 0))
        kv_shape = jax.ShapeDtypeStruct((kv_slots, kv_rows * N_HEADS, HEAD_DIM), F32)
        out_shape = [hm_shape] * 3 + [kv_shape] * 2
        out_specs = [hm_spec] * 3 + [kv_spec] * 2
        if kv_prev is not None:
            assert kv_prev[0].shape == kv_shape.shape
            aliases = {len(args): 3, len(args) + 1: 4}
            in_specs += [pl.BlockSpec(memory_space=pl.ANY)] * 2
            args += list(kv_prev)
    if forget:
        out_shape += [jax.ShapeDtypeStruct((m, N_HEADS), F32)]
        out_specs += [pl.BlockSpec((tm, N_HEADS), lambda i, j: (i, 0))]

    return pl.pallas_call(
        functools.partial(_qkv_kernel, forget=forget, sample=sample, q_scale=q_scale,
                          n_alias=len(aliases)),
        grid=grid,
        in_specs=in_specs,
        out_specs=out_specs,
        out_shape=out_shape,
        scratch_shapes=[pltpu.VMEM((tm, d), BF16)],
        input_output_aliases=aliases,
        compiler_params=_params("arbitrary", "arbitrary"),
        name="qkv_sample" if sample else "qkv_prompt",
    )(*args)


def _cumsum_kernel(x_ref, u_ref, o_ref, carry_ref, *, blocks_per_seq):
    @pl.when(pl.program_id(0) % blocks_per_seq == 0)
    def _():
        carry_ref[...] = jnp.zeros_like(carry_ref)

    x = x_ref[...]
    r, t = x.shape
    s = jnp.dot(_split3(x), u_ref[...], preferred_element_type=F32)
    c = s[:r] + s[r:2 * r] + s[2 * r:] + carry_ref[:, :1]
    o_ref[...] = c
    carry_ref[...] = jnp.broadcast_to(c[:, t - 1:t], carry_ref.shape)


def _cumsum_lanes(x, seq):
    r, l = x.shape
    t = min(SCAN_BLOCK, seq)
    upper = jnp.triu(jnp.ones((t, t), BF16))
    return pl.pallas_call(
        functools.partial(_cumsum_kernel, blocks_per_seq=seq // t),
        grid=(l // t,),
        in_specs=[pl.BlockSpec((r, t), lambda i: (0, i)),
                  pl.BlockSpec((t, t), lambda i: (0, 0))],
        out_specs=pl.BlockSpec((r, t), lambda i: (0, i)),
        out_shape=jax.ShapeDtypeStruct((r, l), F32),
        scratch_shapes=[pltpu.VMEM((r, LANES), F32)],
        compiler_params=_params("arbitrary"),
        name="cumsum_lanes",
    )(x, upper)


def _bias_kernel(shift_ref, xp_ref, xs_ref, bp_ref, bs_ref):
    def toeplitz(row, n_rows, lo, hi):
        base = jnp.broadcast_to(row, (8, ROLL_WIDTH))
        blocks = []
        for g in range(n_rows // 8):
            rolled = pltpu.roll(base, 8 * g, 1, stride=1, stride_axis=0)
            blocks.append(rolled[:, lo:hi])
        return jnp.concatenate(blocks, axis=0)

    row_c = lax.broadcasted_iota(jnp.int32, (BAND_Q, 3 * BAND_Q), 0) // CHUNK
    col = lax.broadcasted_iota(jnp.int32, (BAND_Q, 3 * BAND_Q), 1)
    col_c = (col % BAND_Q) // CHUNK
    visible = ((col >= BAND_Q) | (row_c <= col_c)) & ((col < 2 * BAND_Q) | (col_c <= row_c))
    shift = shift_ref[pl.program_id(0)]
    bp_ref[0] = jnp.where(visible, toeplitz(xp_ref[0], BAND_Q, BAND_Q, ROLL_WIDTH) * LOG2E - shift, NEG)
    bs_ref[0] = toeplitz(xs_ref[0], bs_ref.shape[1], LANES, LANES + bs_ref.shape[2])


def _rel_bias_blocks(table, n_new, shift):
    assert 2 * BAND_Q == WINDOW and 3 * BAND_Q + BAND_Q == ROLL_WIDTH
    rev = table[:, ::-1]
    n_rel = table.shape[1]
    lead_p = 3 * BAND_Q - MAX_REL
    lead_s = WINDOW + LANES - MAX_REL
    row_p = jnp.pad(rev, ((0, 0), (lead_p, ROLL_WIDTH - n_rel - lead_p)), mode="edge")
    row_s = jnp.pad(rev, ((0, 0), (lead_s, ROLL_WIDTH - n_rel - lead_s)), mode="edge")
    h = table.shape[0]
    ws = WINDOW + LANES
    return pl.pallas_call(
        _bias_kernel,
        grid=(h,),
        in_specs=[pl.BlockSpec(memory_space=pltpu.SMEM),
                  pl.BlockSpec((1, 1, ROLL_WIDTH), lambda i: (i, 0, 0)),
                  pl.BlockSpec((1, 1, ROLL_WIDTH), lambda i: (i, 0, 0))],
        out_specs=[pl.BlockSpec((1, BAND_Q, 3 * BAND_Q), lambda i: (i, 0, 0)),
                   pl.BlockSpec((1, n_new, ws), lambda i: (i, 0, 0))],
        out_shape=[jax.ShapeDtypeStruct((h, BAND_Q, 3 * BAND_Q), F32),
                   jax.ShapeDtypeStruct((h, n_new, ws), F32)],
        compiler_params=_params("arbitrary"),
        name="rel_bias",
    )(shift, row_p.reshape(h, 1, ROLL_WIDTH), row_s.reshape(h, 1, ROLL_WIDTH))


def _band_kernel(shifted_ref, q_ref, k_ref, v_ref, b_ref, *rest, batch, n_cast):
    o_ref = rest[n_cast]
    _cast_blocks(rest[:n_cast], rest[n_cast + 1:])
    tq = BAND_Q
    n_sub = q_ref.shape[1] // tq
    step = pl.program_id(1)

    def attend(shifted):
        for sub in range(n_sub):
            i = step * n_sub + sub
            q = q_ref[0, sub * tq:(sub + 1) * tq, :]
            s_parts, v_parts = [], []
            for t in range(3):
                start = pl.multiple_of(jnp.maximum(i - 2 + t, 0) * tq, tq)
                kb = k_ref[0, pl.ds(start, tq), :]
                v_parts.append(v_ref[0, pl.ds(start, tq), :])
                s = lax.dot_general(q, kb, _NT, preferred_element_type=F32)
                s = s + b_ref[0, :, t * tq:(t + 1) * tq]
                if sub + t < 2:
                    s = s + jnp.where(i + t >= 2, 0.0, NEG)
                s_parts.append(s)
            s = jnp.concatenate(s_parts, axis=1)
            if not shifted:
                s = s - jnp.max(s, axis=-1, keepdims=True)
            p = jnp.exp2(s)
            l = jnp.sum(p, axis=-1, keepdims=True)
            pb = p.astype(BF16)
            acc = jnp.dot(pb[:, :tq], v_parts[0], preferred_element_type=F32)
            acc += jnp.dot(pb[:, tq:2 * tq], v_parts[1], preferred_element_type=F32)
            acc += jnp.dot(pb[:, 2 * tq:], v_parts[2], preferred_element_type=F32)
            o_ref[0, sub * tq:(sub + 1) * tq, :] = (acc / l).astype(o_ref.dtype)

    shifted = shifted_ref[pl.program_id(0) // batch] != 0
    pl.when(shifted)(lambda: attend(True))
    pl.when(jnp.logical_not(shifted))(lambda: attend(False))


def _band_attention(qb, kb, vb, bias, shifted, batch, seq, cast=()):
    h, m, hd = qb.shape
    step = min(BAND_STEP, seq)
    nq = seq // step
    c_in, c_args, c_out, c_shape = _cast_plan(cast, h * batch * nq, lambda g, i: g * nq + i)
    return pl.pallas_call(
        functools.partial(_band_kernel, batch=batch, n_cast=len(cast)),
        grid=(h * batch, nq),
        in_specs=[pl.BlockSpec(memory_space=pltpu.SMEM),
                  pl.BlockSpec((1, step, hd), lambda g, i: (g // batch, (g % batch) * nq + i, 0)),
                  pl.BlockSpec((1, seq, hd), lambda g, i: (g // batch, g % batch, 0)),
                  pl.BlockSpec((1, seq, hd), lambda g, i: (g // batch, g % batch, 0)),
                  pl.BlockSpec((1, BAND_Q, 3 * BAND_Q), lambda g, i: (g // batch, 0, 0))] + c_in,
        out_specs=[pl.BlockSpec((1, step, hd),
                                lambda g, i: (g // batch, (g % batch) * nq + i, 0))] + c_out,
        out_shape=[jax.ShapeDtypeStruct((h, m, hd), BF16)] + c_shape,
        compiler_params=_params("arbitrary", "arbitrary"),
        name="band_attention",
    )(shifted, qb, kb, vb, bias, *c_args)


def _band_sample_kernel(q_ref, kn_ref, vn_ref, kc_ref, vc_ref, b_ref, o_ref):
    n = q_ref.shape[0]
    lc = kc_ref.shape[0] // N_HEADS
    for h in range(N_HEADS):
        sl = slice(h * HEAD_DIM, (h + 1) * HEAD_DIM)
        rows = _head_rows(h, lc)
        q = q_ref[:, sl].astype(BF16)
        s_c = lax.dot_general(q, kc_ref[rows, :].astype(BF16), _NT, preferred_element_type=F32)
        s_n = lax.dot_general(q, kn_ref[:, sl].astype(BF16), _NT, preferred_element_type=F32)
        s_c = s_c + b_ref[h, :, :lc]
        s_n = s_n + b_ref[h, :, lc:lc + n]
        m = jnp.maximum(jnp.max(s_c, axis=-1, keepdims=True), jnp.max(s_n, axis=-1, keepdims=True))
        p_c = jnp.exp(s_c - m)
        p_n = jnp.exp(s_n - m)
        l = jnp.sum(p_c, axis=-1, keepdims=True) + jnp.sum(p_n, axis=-1, keepdims=True)
        acc = jnp.dot(p_c.astype(BF16), vc_ref[rows, :].astype(BF16), preferred_element_type=F32)
        acc += jnp.dot(p_n.astype(BF16), vn_ref[:, sl].astype(BF16), preferred_element_type=F32)
        o_ref[h] = (acc / l).astype(o_ref.dtype)


def _band_attention_sample(q, k_new, v_new, k_cache, v_cache, idx, bias, n):
    m, d = q.shape
    _, b, rows, hd = k_cache.shape
    tok = pl.BlockSpec((n, d), lambda i: (i, 0))
    cache = pl.BlockSpec((None, None, rows, hd), lambda i: (idx, i, 0, 0))
    return pl.pallas_call(
        _band_sample_kernel,
        grid=(b,),
        in_specs=[tok, tok, tok, cache, cache,
                  pl.BlockSpec(bias.shape, lambda i: (0, 0, 0))],
        out_specs=pl.BlockSpec((N_HEADS, n, HEAD_DIM), lambda i: (0, i, 0)),
        out_shape=jax.ShapeDtypeStruct((N_HEADS, m, HEAD_DIM), BF16),
        compiler_params=_params("arbitrary"),
        name="band_attention_sample",
    )(q, k_new, v_new, k_cache, v_cache, bias)


def _online_softmax_step(s, carry, v):
    m, l, acc = carry
    m_new = jnp.maximum(m, jnp.max(s, axis=-1, keepdims=True))
    alpha = jnp.exp2(m - m_new)
    p = jnp.exp2(s - m_new)
    l = alpha * l + jnp.sum(p, axis=-1, keepdims=True)
    acc = alpha * acc + jnp.dot(p.astype(BF16), v, preferred_element_type=F32)
    return m_new, l, acc


def _fox_kernel(reach_ref, q_ref, k_ref, v_ref, c_ref, *rest, n_stack, n_cast):
    layer_k, layer_v = rest[:n_stack], rest[n_stack:2 * n_stack]
    rest = rest[2 * n_stack:]
    o_ref = rest[n_cast]
    _cast_blocks(rest[:n_cast], rest[len(rest) - n_cast:])
    if n_stack:
        k_stack, v_stack = rest[n_cast + 1:n_cast + 3]
        for t in range(n_stack):
            k_stack[t] = layer_k[t][0]
            v_stack[t] = layer_v[t][0]

    i = pl.program_id(1)
    tq, hd = q_ref.shape[1], q_ref.shape[2]
    blk = min(FOX_BLOCK, tq)
    tk = min(FOX_CHUNK, tq)
    n, per = tq // blk, tk // blk
    qs = [q_ref[0, a * blk:(a + 1) * blk, :] for a in range(n)]
    reach = reach_ref[0]
    base = pl.multiple_of(i * tq, tq)
    causal = (lax.broadcasted_iota(jnp.int32, (blk, blk), 1)
              <= lax.broadcasted_iota(jnp.int32, (blk, blk), 0))

    def c_rows(first_block, count):
        return [c_ref[0, pl.ds(first_block + t, 1), :] * LOG2E for t in range(count)]

    def c_min_blocks(first_block, count):
        mins = [jnp.min(c, axis=-1, keepdims=True) for c in c_rows(first_block, count)]
        return functools.reduce(jnp.minimum, mins)

    def scores(q, start, size):
        k = k_ref[0, pl.ds(start, size), :]
        return (lax.dot_general(q, k, _NT, preferred_element_type=F32),
                v_ref[0, pl.ds(start, size), :])

    def minus_rows(s, rows):
        return jnp.concatenate([s[:, t * blk:(t + 1) * blk] - r for t, r in enumerate(rows)], axis=1)

    def mask_own(s, a):
        own = jnp.where(causal, s[:, a * blk:], NEG)
        return own if a == 0 else jnp.concatenate([s[:, :a * blk], own], axis=1)

    def finish(a, l, acc):
        o_ref[0, a * blk:(a + 1) * blk, :] = (acc / l).astype(o_ref.dtype)

    def attend_exact():
        def full_chunk(j, carry):
            start = pl.multiple_of(j * tk, tk)
            out = []
            for a in range(n):
                s, v = scores(qs[a], start, tk)
                out.append(_online_softmax_step(minus_rows(s, c_rows(j * per, per)), carry[a], v))
            return tuple(out)

        init = (jnp.full((blk, 1), NEG, F32), jnp.zeros((blk, 1), F32), jnp.zeros((blk, hd), F32))
        carry = lax.fori_loop(0, i * (tq // tk), full_chunk, (init,) * n)
        for a in range(n):
            s, v = scores(qs[a], base, (a + 1) * blk)
            s = mask_own(minus_rows(s, c_rows(i * n, a + 1)), a)
            _, l, acc = _online_softmax_step(s, carry[a], v)
            finish(a, l, acc)

    def attend_bounded():
        def update(s, m_new, carry, v):
            m, l, acc = carry
            alpha = jnp.exp2(m - m_new)
            p = jnp.exp2(s)
            l = alpha * l + jnp.sum(p, axis=-1, keepdims=True)
            return m_new, l, alpha * acc + jnp.dot(p.astype(BF16), v, preferred_element_type=F32)

        def full_chunk(j, carry):
            start = pl.multiple_of(j * tk, tk)
            rows = c_rows(j * per, per)
            m_new = jnp.maximum(carry[0][0], reach - c_min_blocks(j * per, per))
            shifted = [r + m_new for r in rows]
            out = []
            for a in range(n):
                s, v = scores(qs[a], start, tk)
                out.append(update(minus_rows(s, shifted), m_new, carry[a], v))
            return tuple(out)

        init = (jnp.full((1, 1), NEG, F32), jnp.zeros((blk, 1), F32), jnp.zeros((blk, hd), F32))
        carry = lax.fori_loop(0, i * (tq // tk), full_chunk, (init,) * n)
        for a in range(n):
            rows = c_rows(i * n, a + 1)
            c_vis = jnp.min(jnp.where(causal, rows[-1], -NEG), axis=-1, keepdims=True)
            c_seen = c_vis if a == 0 else jnp.minimum(c_vis, c_min_blocks(i * n, a))
            m_new = jnp.maximum(carry[a][0], reach - c_seen)
            s, v = scores(qs[a], base, (a + 1) * blk)
            s = mask_own(minus_rows(s, rows) - m_new, a)
            _, l, acc = update(s, m_new, carry[a], v)
            finish(a, l, acc)

    use_bound = 2.0 * reach <= MAX_SLACK
    pl.when(use_bound)(attend_bounded)
    pl.when(jnp.logical_not(use_bound))(attend_exact)


def _fox_attention(qb, kb, vb, c_rows, reach, batch, seq, stack_kv=(), cast=()):
    h, m, hd = qb.shape
    t = min(FOX_STEP, seq)
    nq = seq // t
    blk = min(FOX_BLOCK, t)
    c_blocks = c_rows.reshape(h, m // blk, blk)
    in_specs = [pl.BlockSpec(memory_space=pltpu.SMEM),
                pl.BlockSpec((1, t, hd), lambda g, i: (g // batch, (g % batch) * nq + i, 0)),
                pl.BlockSpec((1, seq, hd), lambda g, i: (g // batch, g % batch, 0)),
                pl.BlockSpec((1, seq, hd), lambda g, i: (g // batch, g % batch, 0)),
                pl.BlockSpec((1, seq // blk, blk), lambda g, i: (g // batch, g % batch, 0))]
    out_specs = [pl.BlockSpec((1, t, hd), lambda g, i: (g // batch, (g % batch) * nq + i, 0))]
    out_shape = [jax.ShapeDtypeStruct((h, m, hd), BF16)]
    n_stack = len(stack_kv)
    extra = []
    if n_stack:
        rows = stack_kv[0][0].shape[1]
        steps = h * batch * nq
        assert rows % (steps * 8) == 0
        per = rows // steps
        in_specs += [pl.BlockSpec((1, per, hd), lambda g, i: (0, g * nq + i, 0))] * (2 * n_stack)
        extra = [kv[0] for kv in stack_kv] + [kv[1] for kv in stack_kv]
        out_specs += [pl.BlockSpec((n_stack, per, hd), lambda g, i: (0, g * nq + i, 0))] * 2
        out_shape += [jax.ShapeDtypeStruct((n_stack, rows, hd), F32)] * 2
    c_in, c_args, c_out, c_shape = _cast_plan(cast, h * batch * nq, lambda g, i: g * nq + i)
    return pl.pallas_call(
        functools.partial(_fox_kernel, n_stack=n_stack, n_cast=len(cast)),
        grid=(h * batch, nq),
        in_specs=in_specs + c_in,
        out_specs=out_specs + c_out,
        out_shape=out_shape + c_shape,
        compiler_params=_params("arbitrary", "arbitrary", vmem=BIG_VMEM_LIMIT),
        name="fox_attention",
    )(reach.reshape(1), qb, kb, vb, c_blocks, *extra, *c_args)


def _fox_sample_kernel(q_ref, kn_ref, vn_ref, kc_ref, vc_ref, cc_ref, lfn_ref, u_ref, o_ref):
    n = q_ref.shape[0]
    c_cache = cc_ref[0]
    lc = c_cache.shape[1]
    s3 = jnp.dot(_split3(lfn_ref[0]), u_ref[...], preferred_element_type=F32)
    c_new = s3[:N_HEADS] + s3[N_HEADS:2 * N_HEADS] + s3[2 * N_HEADS:] + c_cache[:, lc - 1:lc]
    row = lax.broadcasted_iota(jnp.int32, (n, n), 0)
    col = lax.broadcasted_iota(jnp.int32, (n, n), 1)
    for h in range(N_HEADS):
        sl = slice(h * HEAD_DIM, (h + 1) * HEAD_DIM)
        rows = _head_rows(h, lc)
        q = q_ref[:, sl].astype(BF16)
        s_c = lax.dot_general(q, kc_ref[rows, :].astype(BF16), _NT, preferred_element_type=F32)
        s_n = lax.dot_general(q, kn_ref[:, sl].astype(BF16), _NT, preferred_element_type=F32)
        s_c = s_c - c_cache[h:h + 1, :]
        s_n = jnp.where(col <= row, s_n - c_new[h:h + 1, :], NEG)
        m = jnp.maximum(jnp.max(s_c, axis=-1, keepdims=True), jnp.max(s_n, axis=-1, keepdims=True))
        p_c = jnp.exp(s_c - m)
        p_n = jnp.exp(s_n - m)
        l = jnp.sum(p_c, axis=-1, keepdims=True) + jnp.sum(p_n, axis=-1, keepdims=True)
        acc = jnp.dot(p_c.astype(BF16), vc_ref[rows, :].astype(BF16), preferred_element_type=F32)
        acc += jnp.dot(p_n.astype(BF16), vn_ref[:, sl].astype(BF16), preferred_element_type=F32)
        o_ref[h] = (acc / l).astype(o_ref.dtype)


def _fox_attention_sample(q, k_new, v_new, k_cache, v_cache, idx, c_cache, lf_new, n):
    m, d = q.shape
    _, b, rows, hd = k_cache.shape
    lc = rows // N_HEADS
    tok = pl.BlockSpec((n, d), lambda i: (i, 0))
    cache = pl.BlockSpec((None, None, rows, hd), lambda i: (idx, i, 0, 0))
    upper = jnp.triu(jnp.ones((n, n), BF16))
    return pl.pallas_call(
        _fox_sample_kernel,
        grid=(b,),
        in_specs=[tok, tok, tok, cache, cache,
                  pl.BlockSpec((1, N_HEADS, lc), lambda i: (i, 0, 0)),
                  pl.BlockSpec((1, N_HEADS, n), lambda i: (i, 0, 0)),
                  pl.BlockSpec((n, n), lambda i: (0, 0))],
        out_specs=pl.BlockSpec((N_HEADS, n, HEAD_DIM), lambda i: (0, i, 0)),
        out_shape=jax.ShapeDtypeStruct((N_HEADS, m, HEAD_DIM), BF16),
        compiler_params=_params("arbitrary"),
        name="fox_attention_sample",
    )(q, k_new, v_new, k_cache, v_cache, c_cache, lf_new, upper)


def _out_proj_kernel(a_ref, w_ref, x_ref, o_ref):
    a = jnp.concatenate([a_ref[h] for h in range(N_HEADS)], axis=1)
    o_ref[...] = x_ref[...] + jnp.dot(a, w_ref[...], preferred_element_type=F32)


def _out_proj(attn, w_o, x):
    m, d = x.shape
    tm = min(FFN_ROW_TILE, m)
    return pl.pallas_call(
        _out_proj_kernel,
        grid=(m // tm,),
        in_specs=[pl.BlockSpec((N_HEADS, tm, HEAD_DIM), lambda i: (0, i, 0)),
                  pl.BlockSpec((d, d), lambda i: (0, 0), pipeline_mode=pl.Buffered(1)),
                  pl.BlockSpec((tm, d), lambda i: (i, 0))],
        out_specs=pl.BlockSpec((tm, d), lambda i: (i, 0)),
        out_shape=jax.ShapeDtypeStruct((m, d), F32),
        compiler_params=_params("arbitrary"),
        name="out_proj",
    )(attn, w_o, x)


def _ffn_kernel(x_ref, g_ref, wg_ref, wu_ref, wd_ref, o_ref, xn_ref):
    @pl.when(pl.program_id(1) == 0)
    def _():
        x = x_ref[...]
        ms = jnp.mean(x * x, axis=-1, keepdims=True)
        xn_ref[...] = (x * lax.rsqrt(ms + EPS) * g_ref[...]).astype(BF16)
        o_ref[...] = x

    xn = xn_ref[...]
    gate = jnp.dot(xn, wg_ref[...], preferred_element_type=F32)
    up = jnp.dot(xn, wu_ref[...], preferred_element_type=F32)
    hidden = (gate * jax.nn.sigmoid(gate) * up).astype(BF16)
    o_ref[...] += jnp.dot(hidden, wd_ref[...], preferred_element_type=F32)


def _ffn(x, g_ffn, w_gate, w_up, w_down):
    m, d = x.shape
    f = w_gate.shape[1]
    tm = min(FFN_ROW_TILE, m)
    tf = FFN_TILE
    return pl.pallas_call(
        _ffn_kernel,
        grid=(m // tm, f // tf),
        in_specs=[pl.BlockSpec((tm, d), lambda i, j: (i, 0)),
                  pl.BlockSpec((1, d), lambda i, j: (0, 0)),
                  pl.BlockSpec((d, tf), lambda i, j: (0, j)),
                  pl.BlockSpec((d, tf), lambda i, j: (0, j)),
                  pl.BlockSpec((tf, d), lambda i, j: (j, 0))],
        out_specs=pl.BlockSpec((tm, d), lambda i, j: (i, 0)),
        out_shape=jax.ShapeDtypeStruct((m, d), F32),
        scratch_shapes=[pltpu.VMEM((tm, d), BF16)],
        compiler_params=_params("arbitrary", "arbitrary", vmem=BIG_VMEM_LIMIT),
        name="ffn",
    )(x, g_ffn.reshape(1, d), w_gate, w_up, w_down)


def kernel(x_prompt, x_sample, cache_a_k, cache_a_v, cache_b_k, cache_b_v, cache_b_logf,
           g_attn, w_qkv, g_q, g_k, w_o, rel_table, w_f, b_f, g_ffn, w_gate, w_up, w_down):
    batch, seq, d = x_prompt.shape
    dec_batch, n_new, _ = x_sample.shape
    depth = g_attn.shape[0]
    n_a, n_b = cache_a_k.shape[0], cache_b_k.shape[0]
    keep = min(WINDOW, seq)
    h, hd = N_HEADS, HEAD_DIM
    la, past = cache_a_k.shape[2], cache_b_k.shape[2]
    assert d == h * hd and keep == WINDOW and la == WINDOW

    w_f_b = w_f.astype(BF16)
    w_qkv_l = w_qkv[0].astype(BF16)

    cache_a_k = cache_a_k.reshape(n_a, dec_batch, la * h, hd)
    cache_a_v = cache_a_v.reshape(n_a, dec_batch, la * h, hd)
    cache_b_k = cache_b_k.reshape(n_b, dec_batch, past * h, hd)
    cache_b_v = cache_b_v.reshape(n_b, dec_batch, past * h, hd)

    yp = x_prompt.reshape(batch * seq, d)
    ys = x_sample.reshape(dec_batch * n_new, d)
    a_kv = [jnp.zeros((n_a, batch * keep * h, hd), F32) for _ in range(2)]
    b_layers, b_kv = [], None
    b_fp, a_ks, a_vs, b_ks, b_vs, b_fs = [], [], [], [], [], []

    for layer in range(depth):
        idx = layer // 2
        common = (g_attn[layer], w_qkv_l, g_q[layer], g_k[layer])
        cast = [(w_o, layer), (w_gate, layer), (w_up, layer), (w_down, layer)]
        if layer + 1 < depth:
            cast.append((w_qkv, layer + 1))
        reach = _qk_reach(g_q[layer], g_k[layer])
        if layer % 2 == 0:
            qb, kb, vb, *a_kv = _qkv_proj(
                yp, *common, sample=False, q_scale=SCALE * LOG2E, kv_last=keep, seq=seq,
                kv_slots=n_a, kv_slot=idx, kv_prev=a_kv)
            qs, ks, vs = _qkv_proj(ys, *common, sample=True)
            b_hi = jnp.max(rel_table[idx], axis=1) * LOG2E
            b_lo = jnp.min(rel_table[idx], axis=1) * LOG2E
            shifted = 2.0 * reach + (b_hi - b_lo) <= MAX_SLACK
            bias_p, bias_s = _rel_bias_blocks(rel_table[idx], n_new,
                                              jnp.where(shifted, reach + b_hi, 0.0))
            mp, *cast_w = _band_attention(qb, kb, vb, bias_p, shifted.astype(jnp.int32), batch, seq,
                                          cast=cast)
            ms = _band_attention_sample(qs, ks, vs, cache_a_k, cache_a_v, idx, bias_s, n_new)
            a_ks.append(ks)
            a_vs.append(vs)
        else:
            fw = (w_f_b[idx], b_f[idx])
            qb, kb, vb, k_all, v_all, lf = _qkv_proj(
                yp, *common, fw, sample=False, q_scale=SCALE * LOG2E, kv_slots=1)
            b_layers.append((k_all, v_all))
            qs, ks, vs, lfs = _qkv_proj(ys, *common, fw, sample=True)
            last_b = idx == n_b - 1
            mp, *more = _fox_attention(qb, kb, vb, _cumsum_lanes(lf.T, seq), reach, batch, seq,
                                       stack_kv=b_layers if last_b else (), cast=cast)
            b_kv, cast_w = more[:len(more) - len(cast)], more[len(more) - len(cast):]
            lf_cache = cache_b_logf[idx].transpose(0, 2, 1).reshape(dec_batch * h, past)
            c_cache = _cumsum_lanes(lf_cache, past).reshape(dec_batch, h, past)
            lf_new = lfs.reshape(dec_batch, n_new, h).transpose(0, 2, 1)
            ms = _fox_attention_sample(qs, ks, vs, cache_b_k, cache_b_v, idx, c_cache, lf_new, n_new)
            b_fp.append(lf.reshape(batch, seq, h))
            b_ks.append(ks)
            b_vs.append(vs)
            b_fs.append(lfs.reshape(dec_batch, n_new, h))
        w_o_l, w_gate_l, w_up_l, w_down_l = cast_w[:4]
        if layer + 1 < depth:
            w_qkv_l = cast_w[4]
        yp = _out_proj(mp, w_o_l, yp)
        ys = _out_proj(ms, w_o_l, ys)
        yp = _ffn(yp, g_ffn[layer], w_gate_l, w_up_l, w_down_l)
        ys = _ffn(ys, g_ffn[layer], w_gate_l, w_up_l, w_down_l)

    def new_kv(parts):
        return jnp.stack(parts).reshape(len(parts), dec_batch, n_new, h, hd)

    return (yp.reshape(batch, seq, d), ys.reshape(dec_batch, n_new, d),
            a_kv[0].reshape(n_a, batch, keep, h, hd), a_kv[1].reshape(n_a, batch, keep, h, hd),
            b_kv[0].reshape(n_b, batch, seq, h, hd), b_kv[1].reshape(n_b, batch, seq, h, hd),
            jnp.stack(b_fp), new_kv(a_ks), new_kv(a_vs), new_kv(b_ks), new_kv(b_vs), jnp.stack(b_fs))
```
